```python
import jax, jax.numpy as jnp
from jax import lax
import numpy as np

D_MODEL = 1024
BATCH = 16
SEQ = 4096
DEPTH = 1

MEM_LEN = 256
CHUNK = 64
EPS = 1e-6
MLSTM_HEADS = 4
MLSTM_W = D_MODEL // 2
MLSTM_HD = MLSTM_W // MLSTM_HEADS
CONV_W = 4
F_BIAS_LO = 3.0
F_BIAS_HI = 6.0
GLA_HEADS = 4
GLA_WV = D_MODEL // 2
GLA_WK = GLA_WV // 2
GLA_DK = GLA_WK // GLA_HEADS
GLA_DV = GLA_WV // GLA_HEADS
GLA_RANK = 16
GLA_TAU = 16.0
XATTN_HEADS = 4
XATTN_HD = D_MODEL // XATTN_HEADS
D_FF = 4 * D_MODEL

IN_SPLITS = [MLSTM_W, MLSTM_W, MLSTM_W, MLSTM_W, MLSTM_HEADS, MLSTM_HEADS,
             GLA_WK, GLA_WK, GLA_WV, GLA_WV, GLA_RANK]
D_IN = sum(IN_SPLITS)
IN_SPLIT_IDX = [int(c) for c in np.cumsum(IN_SPLITS)[:-1]]
MIX_W = MLSTM_W + GLA_WV

kernel_name = 'hybrid_mlstm_gla_block'


def rmsnorm(x, g):
    xf = x.astype(jnp.float32)
    y = xf * lax.rsqrt(jnp.mean(xf * xf, axis=-1, keepdims=True) + EPS)
    return y.astype(x.dtype) * g


def head_rmsnorm(h, g):
    y = h * lax.rsqrt(jnp.mean(h * h, axis=-1, keepdims=True) + EPS)
    B, S = y.shape[:2]
    return y.reshape(B, S, -1) * g.astype(jnp.float32)


def causal_conv(x, w, b):
    S = x.shape[1]
    xp = jnp.pad(x, ((0, 0), (CONV_W - 1, 0), (0, 0)))
    y = xp[:, 0:S] * w[0]
    for j in range(1, CONV_W):
        y = y + xp[:, j:j + S] * w[j]
    return y + b


def to_chunks(t, n_heads):
    B, S = t.shape[:2]
    return t.reshape(B, S // CHUNK, CHUNK, n_heads, -1).transpose(1, 0, 3, 2, 4)


def gate_chunks(t):
    B, S, H = t.shape
    return t.reshape(B, S // CHUNK, CHUNK, H).transpose(1, 0, 3, 2)


def from_chunks(t):
    NC, B, H, L, d = t.shape
    return t.transpose(1, 0, 3, 2, 4).reshape(B, NC * L, H, d)


def mlstm_chunkwise(q, k, v, log_i, log_f):
    NC, B, H, L, d = q.shape
    causal = jnp.tril(jnp.ones((L, L), dtype=bool))

    def step(carry, inp):
        C, n, m = carry
        qc, kc, vc, li, lf = inp
        b = jnp.cumsum(lf, axis=-1)
        D = jnp.where(causal, b[..., :, None] - b[..., None, :] + li[..., None, :], -jnp.inf)
        inter = b + m[..., None]
        m_t = jnp.maximum(jnp.max(D, axis=-1), inter)
        s = jnp.einsum('bhtd,bhsd->bhts', qc, kc) * jnp.exp(D - m_t[..., None])
        w_inter = jnp.exp(inter - m_t)
        num = (jnp.einsum('bhts,bhse->bhte', s, vc)
               + w_inter[..., None] * jnp.einsum('bhtd,bhde->bhte', qc, C))
        den = jnp.sum(s, axis=-1) + w_inter * jnp.einsum('bhtd,bhd->bht', qc, n)
        h = num / jnp.maximum(jnp.abs(den), jnp.exp(-m_t))[..., None]
        bL = b[..., -1]
        g = bL[..., None] - b + li
        m_new = jnp.maximum(bL + m, jnp.max(g, axis=-1))
        decay = jnp.exp(bL + m - m_new)
        wk = jnp.exp(g - m_new[..., None])[..., None] * kc
        C_new = decay[..., None, None] * C + jnp.einsum('bhsd,bhse->bhde', wk, vc)
        n_new = decay[..., None] * n + jnp.sum(wk, axis=2)
        return (C_new, n_new, m_new), h

    init = (jnp.zeros((B, H, d, v.shape[-1]), jnp.float32),
            jnp.zeros((B, H, d), jnp.float32),
            jnp.zeros((B, H), jnp.float32))
    _, h = lax.scan(step, init, (q, k, v, log_i, log_f))
    return h


def gla_chunkwise(q, k, v, log_a):
    NC, B, H, L, dk = q.shape
    causal = jnp.tril(jnp.ones((L, L), dtype=bool))

    def step(Sst, inp):
        qc, kc, vc, la = inp
        b = jnp.cumsum(la, axis=2)
        diff = jnp.where(causal[:, :, None], b[:, :, :, None, :] - b[:, :, None, :, :], -jnp.inf)
        A = jnp.einsum('bhtd,bhsd,bhtsd->bhts', qc, kc, jnp.exp(diff))
        o = (jnp.einsum('bhts,bhse->bhte', A, vc)
             + jnp.einsum('bhtd,bhde->bhte', qc * jnp.exp(b), Sst))
        bL = b[:, :, -1]
        kw = kc * jnp.exp(bL[:, :, None] - b)
        S_new = jnp.exp(bL)[..., None] * Sst + jnp.einsum('bhsd,bhse->bhde', kw, vc)
        return S_new, o

    init = jnp.zeros((B, H, dk, v.shape[-1]), jnp.float32)
    _, o = lax.scan(step, init, (q, k, v, log_a))
    return o


def setup_inputs(seed: int = 0) -> dict:
    key = jax.random.key(seed)
    ks = jax.random.split(key, 24)
    f32 = jnp.float32

    def nrm(k, shape, scale):
        return jax.random.normal(k, shape, f32) * scale

    def gain(k, shape):
        return 1.0 + 0.01 * jax.random.normal(k, shape, f32)

    f_bias = (jnp.linspace(F_BIAS_LO, F_BIAS_HI, MLSTM_HEADS, dtype=f32)[None]
              + nrm(ks[6], (DEPTH, MLSTM_HEADS), 0.01))
    return {
        'x': nrm(ks[0], (BATCH, SEQ, D_MODEL), 1.0),
        'mem': nrm(ks[1], (BATCH, MEM_LEN, D_MODEL), 1.0),
        'mix_norm_g': gain(ks[2], (DEPTH, D_MODEL)),
        'w_in': nrm(ks[3], (DEPTH, D_MODEL, D_IN), D_MODEL ** -0.5),
        'conv_w': nrm(ks[4], (DEPTH, CONV_W, 2 * MLSTM_W), CONV_W ** -0.5),
        'conv_b': nrm(ks[5], (DEPTH, 2 * MLSTM_W), 0.01),
        'mlstm_i_b': nrm(ks[7], (DEPTH, MLSTM_HEADS), 0.1),
        'mlstm_f_b': f_bias,
        'mlstm_norm_g': gain(ks[8], (DEPTH, MLSTM_W)),
        'gla_wa2': nrm(ks[9], (DEPTH, GLA_RANK, GLA_WK), GLA_RANK ** -0.5),
        'gla_ba': nrm(ks[10], (DEPTH, GLA_WK), 0.1),
        'gla_norm_g': gain(ks[11], (DEPTH, GLA_WV)),
        'w_out': nrm(ks[12], (DEPTH, MIX_W, D_MODEL), MIX_W ** -0.5),
        'xattn_norm_g': gain(ks[13], (DEPTH, D_MODEL)),
        'mem_norm_g': gain(ks[14], (DEPTH, D_MODEL)),
        'wq_x': nrm(ks[15], (DEPTH, D_MODEL, D_MODEL), D_MODEL ** -0.5),
        'wk_x': nrm(ks[16], (DEPTH, D_MODEL, D_MODEL), D_MODEL ** -0.5),
        'wv_x': nrm(ks[17], (DEPTH, D_MODEL, D_MODEL), D_MODEL ** -0.5),
        'wo_x': nrm(ks[18], (DEPTH, D_MODEL, D_MODEL), D_MODEL ** -0.5),
        'mlp_norm_g': gain(ks[19], (DEPTH, D_MODEL)),
        'w1': nrm(ks[20], (DEPTH, D_MODEL, D_FF), D_MODEL ** -0.5),
        'w2': nrm(ks[21], (DEPTH, D_FF, D_MODEL), D_FF ** -0.5),
        'final_norm_g': gain(ks[22], (D_MODEL,)),
    }


def reference(x, mem, mix_norm_g, w_in, conv_w, conv_b, mlstm_i_b, mlstm_f_b, mlstm_norm_g,
              gla_wa2, gla_ba, gla_norm_g, w_out, xattn_norm_g, mem_norm_g, wq_x, wk_x, wv_x,
              wo_x, mlp_norm_g, w1, w2, final_norm_g):
    f32 = jnp.float32
    dt = x.dtype
    B, S, _ = x.shape
    M = mem.shape[1]
    for l in range(DEPTH):
        h = rmsnorm(x, mix_norm_g[l])
        u = h @ w_in[l]
        mq, mk, mv, mo, mi, mf, gq, gk, gv, gr, ga = jnp.split(u, IN_SPLIT_IDX, axis=-1)

        qk = jax.nn.silu(causal_conv(jnp.concatenate([mq, mk], axis=-1), conv_w[l], conv_b[l]))
        mq, mk = jnp.split(qk, 2, axis=-1)
        log_i = (mi + mlstm_i_b[l]).astype(f32)
        log_f = jax.nn.log_sigmoid((mf + mlstm_f_b[l]).astype(f32))
        h_m = mlstm_chunkwise(to_chunks((mq * MLSTM_HD ** -0.5).astype(f32), MLSTM_HEADS),
                              to_chunks(mk.astype(f32), MLSTM_HEADS),
                              to_chunks(mv.astype(f32), MLSTM_HEADS),
                              gate_chunks(log_i), gate_chunks(log_f))
        h_m = head_rmsnorm(from_chunks(h_m), mlstm_norm_g[l]).astype(dt) * jax.nn.sigmoid(mo)

        log_a = jax.nn.log_sigmoid((ga @ gla_wa2[l] + gla_ba[l]).astype(f32)) / GLA_TAU
        h_g = gla_chunkwise(to_chunks((gq * GLA_DK ** -0.5).astype(f32), GLA_HEADS),
                            to_chunks(gk.astype(f32), GLA_HEADS),
                            to_chunks(gv.astype(f32), GLA_HEADS),
                            to_chunks(log_a, GLA_HEADS))
        h_g = head_rmsnorm(from_chunks(h_g), gla_norm_g[l]).astype(dt) * jax.nn.silu(gr)

        x = x + jnp.concatenate([h_m, h_g], axis=-1) @ w_out[l]

        hq = (rmsnorm(x, xattn_norm_g[l]) @ wq_x[l]).reshape(B, S, XATTN_HEADS, XATTN_HD)
        mn = rmsnorm(mem, mem_norm_g[l])
        mk_ = (mn @ wk_x[l]).reshape(B, M, XATTN_HEADS, XATTN_HD)
        mv_ = (mn @ wv_x[l]).reshape(B, M, XATTN_HEADS, XATTN_HD)
        scores = jnp.einsum('bshd,bmhd->bhsm', hq, mk_).astype(f32) * (XATTN_HD ** -0.5)
        p = jax.nn.softmax(scores, axis=-1).astype(dt)
        att = jnp.einsum('bhsm,bmhd->bshd', p, mv_).reshape(B, S, D_MODEL)
        x = x + att @ wo_x[l]

        hm = rmsnorm(x, mlp_norm_g[l]) @ w1[l]
        x = x + jnp.square(jax.nn.relu(hm)) @ w2[l]
    return rmsnorm(x, final_norm_g)
```

```python
import functools

import jax
import jax.numpy as jnp
from jax import lax
from jax.experimental import pallas as pl
from jax.experimental.pallas import tpu as pltpu

EPS = 1e-6
CHUNK = 64
CONV_W = 4
MLSTM_HEADS = 4
GLA_HEADS = 4
GLA_RANK = 16
GLA_TAU = 16.0
XATTN_HEADS = 4
LANES = 128
VMEM_LIMIT_BYTES = 56 * 1024 * 1024

BF16 = jnp.bfloat16
F32 = jnp.float32


def _rmsnorm(x, g):
    return x * lax.rsqrt(jnp.mean(x * x, axis=-1, keepdims=True) + EPS) * g


def _log_sigmoid(z):
    return jnp.minimum(z, 0.0) - jnp.log1p(jnp.exp(-jnp.abs(z)))


def _dot(a, b):
    return jnp.dot(a, b, preferred_element_type=F32)


def _dot_nt(a, b):
    return lax.dot_general(a, b, (((1,), (1,)), ((), ())), preferred_element_type=F32)


def _dot_tn(a, b):
    return lax.dot_general(a, b, (((0,), (0,)), ((), ())), preferred_element_type=F32)


def _segment_cumsum(x, seg):
    row = lax.broadcasted_iota(jnp.int32, x.shape, 0) & (seg - 1)
    k = 1
    while k < seg:
        shifted = pltpu.roll(x, k, axis=0)
        x = x + jnp.where(row >= k, shifted, 0.0)
        k *= 2
    return x


def _mem_kv_kernel(mem_ref, g_ref, wk_ref, wv_ref, k_ref, v_ref):
    mn = _rmsnorm(mem_ref[0], g_ref[...]).astype(BF16)
    k_ref[0] = _dot(mn, wk_ref[...]).astype(BF16)
    v_ref[0] = _dot(mn, wv_ref[...]).astype(BF16)


def _mem_kv(mem, g, wk, wv):
    B, M, D = mem.shape
    const = lambda b: (0, 0)
    return pl.pallas_call(
        _mem_kv_kernel,
        grid=(B,),
        in_specs=[
            pl.BlockSpec((1, M, D), lambda b: (b, 0, 0)),
            pl.BlockSpec((1, D), const),
            pl.BlockSpec((D, D), const),
            pl.BlockSpec((D, D), const),
        ],
        out_specs=[
            pl.BlockSpec((1, M, D), lambda b: (b, 0, 0)),
            pl.BlockSpec((1, M, D), lambda b: (b, 0, 0)),
        ],
        out_shape=[jax.ShapeDtypeStruct((B, M, D), BF16)] * 2,
        compiler_params=pltpu.CompilerParams(
            dimension_semantics=("arbitrary",), vmem_limit_bytes=VMEM_LIMIT_BYTES),
        name="mem_kv",
    )(mem, g, wk, wv)


def _mixer_kernel(x_ref, g_ref, w_ref, sbias_ref, convw_ref, convb_ref, wa2_ref, ba_ref,
                  mng_ref, gng_ref, wout_ref,
                  o_ref,
                  cbuf, q_s, k_s, v_s, gate_s, gq_s, gk_s, gv_s, bcol_s, a_s, bg_s, hraw_s,
                  c_s, n_s, m_s, st_s, *, ts, mw, gwk, gwv):
    hd = mw // MLSTM_HEADS
    gdk = gwk // GLA_HEADS
    gdv = gwv // GLA_HEADS
    o_v = 2 * mw
    o_o = 3 * mw
    o_gq = 4 * mw
    o_gk = o_gq + gwk
    o_gv = o_gk + gwk
    o_gr = o_gv + gwv
    o_sm = o_gr + gwv

    @pl.when(pl.program_id(1) == 0)
    def _():
        cbuf[0:8, :] = jnp.zeros((8, cbuf.shape[1]), F32)
        c_s[...] = jnp.zeros(c_s.shape, F32)
        n_s[...] = jnp.zeros(n_s.shape, F32)
        m_s[...] = jnp.zeros(m_s.shape, F32)
        st_s[...] = jnp.zeros(st_s.shape, F32)

    x = x_ref[0]
    h = _rmsnorm(x, g_ref[...]).astype(BF16)

    cbuf[pl.ds(8, ts), :] = _dot(h, w_ref[:, 0:o_v])
    v_s[...] = _dot(h, w_ref[:, o_v:o_o]).astype(BF16)
    gate_s[:, 0:mw] = jax.nn.sigmoid(_dot(h, w_ref[:, o_o:o_gq]))
    gq_s[...] = _dot(h, w_ref[:, o_gq:o_gk])
    gk_s[...] = _dot(h, w_ref[:, o_gk:o_gv])
    gv_s[...] = _dot(h, w_ref[:, o_gv:o_gr]).astype(BF16)
    gr = _dot(h, w_ref[:, o_gr:o_sm])
    gate_s[:, mw:mw + gwv] = gr * jax.nn.sigmoid(gr)
    small = _dot(h, w_ref[:, o_sm:o_sm + LANES]) + sbias_ref[...]

    y = convb_ref[...] + cbuf[pl.ds(8, ts), :] * convw_ref[3:4, :]
    y = y + cbuf[pl.ds(7, ts), :] * convw_ref[2:3, :]
    y = y + cbuf[pl.ds(6, ts), :] * convw_ref[1:2, :]
    y = y + cbuf[pl.ds(5, ts), :] * convw_ref[0:1, :]
    tail = cbuf[pl.ds(ts, 8), :]
    cbuf[0:8, :] = tail
    y = y * jax.nn.sigmoid(y)
    q_s[...] = (y[:, 0:mw] * (hd ** -0.5)).astype(BF16)
    k_s[...] = y[:, mw:2 * mw]

    cs = _segment_cumsum(_log_sigmoid(small), CHUNK)
    bcol = pltpu.roll(cs, LANES - MLSTM_HEADS, axis=1)
    bcol_s[...] = bcol
    a_s[pl.ds(0, ts), :] = small - bcol
    a_s[pl.ds(ts, CHUNK), :] = jnp.zeros((CHUNK, LANES), F32)

    la = _log_sigmoid(_dot(small.astype(BF16), wa2_ref[...]) + ba_ref[...]) * (1.0 / GLA_TAU)
    bg_s[...] = _segment_cumsum(la, CHUNK)

    ti = lax.broadcasted_iota(jnp.int32, (CHUNK, CHUNK), 0)
    si = lax.broadcasted_iota(jnp.int32, (CHUNK, CHUNK), 1)
    causal = ti >= si
    lane = lax.broadcasted_iota(jnp.int32, (CHUNK, gwk), 1)

    def chunk_body(c, carry):
        r0 = pl.multiple_of(c * CHUNK, CHUNK)
        rows = pl.ds(r0, CHUNK)

        bc = bcol_s[rows, :]
        ac = a_s[rows, :]
        at = a_s[pl.ds(r0, 2 * CHUNK), :].T
        for hh in range(MLSTM_HEADS):
            cols = slice(hh * hd, (hh + 1) * hd)
            qh = q_s[rows, cols]
            kh = k_s[rows, cols]
            vh = v_s[rows, cols]
            b_col = bc[:, hh:hh + 1]
            a_col = ac[:, hh:hh + 1]
            a_row = at[hh:hh + 1, 0:CHUNK]
            b_last = bc[CHUNK - 1:CHUNK, hh:hh + 1]
            m_prev = m_s[hh:hh + 1, 0:1]
            c_prev = c_s[hh]
            n_prev = n_s[hh:hh + 1, :]

            dmat = jnp.where(causal, b_col + a_row, -jnp.inf)
            inter = b_col + m_prev
            m_t = jnp.maximum(jnp.max(dmat, axis=1, keepdims=True), inter)
            s = _dot_nt(qh, kh.astype(BF16)) * jnp.exp(dmat - m_t)
            w_inter = jnp.exp(inter - m_t)
            num = _dot(s.astype(BF16), vh) + w_inter * _dot(qh, c_prev.astype(BF16))
            qn = jnp.sum(qh.astype(F32) * n_prev, axis=1, keepdims=True)
            den = jnp.sum(s, axis=1, keepdims=True) + w_inter * qn
            hraw_s[rows, cols] = num / jnp.maximum(jnp.abs(den), jnp.exp(-m_t))

            g_col = b_last + a_col
            m_new = jnp.maximum(b_last + m_prev, jnp.max(g_col, axis=0, keepdims=True))
            decay = jnp.exp(b_last + m_prev - m_new)
            wk = jnp.exp(g_col - m_new) * kh
            c_s[hh] = decay * c_prev + _dot_tn(wk.astype(BF16), vh)
            n_s[hh:hh + 1, :] = decay * n_prev + jnp.sum(wk, axis=0, keepdims=True)
            m_s[hh:hh + 1, :] = jnp.broadcast_to(m_new, (1, LANES))

        bg = bg_s[rows, :]
        bg_last = bg[CHUNK - 1:CHUNK, :]
        gq = gq_s[rows, :]
        gk = gk_s[rows, :]
        gv = gv_s[rows, :]
        qe = gq * jnp.exp(bg) * (gdk ** -0.5)
        ke = (gk * jnp.exp(-bg)).astype(BF16)
        kw = gk * jnp.exp(bg_last - bg)
        st_prev = st_s[...]
        qe_heads = []
        kw_heads = []
        for hh in range(GLA_HEADS):
            hm = (lane >= hh * gdk) & (lane < (hh + 1) * gdk)
            qe_heads.append(jnp.where(hm, qe, 0.0).astype(BF16))
            kw_heads.append(jnp.where(hm, kw, 0.0).astype(BF16))
        qe_stack = jnp.concatenate(qe_heads, axis=0)
        a_all = _dot_nt(qe_stack, ke)
        inter_all = _dot_nt(qe_stack, st_prev.astype(BF16))
        upd = jnp.zeros(st_prev.shape, F32)
        for hh in range(GLA_HEADS):
            hrows = slice(hh * CHUNK, (hh + 1) * CHUNK)
            vcols = slice(hh * gdv, (hh + 1) * gdv)
            a_h = jnp.where(causal, a_all[hrows, :], 0.0).astype(BF16)
            hraw_s[rows, mw + hh * gdv:mw + (hh + 1) * gdv] = _dot(a_h, gv[:, vcols]) + inter_all[hrows, :]
            upd = upd + _dot_tn(gv[:, vcols], kw_heads[hh])
        st_s[...] = st_prev * jnp.exp(bg_last) + upd
        return carry

    lax.fori_loop(0, ts // CHUNK, chunk_body, 0)

    parts = []
    for hh in range(MLSTM_HEADS + GLA_HEADS):
        hv = hraw_s[:, hh * hd:(hh + 1) * hd]
        parts.append(hv * lax.rsqrt(jnp.mean(hv * hv, axis=-1, keepdims=True) + EPS))
    normed = jnp.concatenate(parts, axis=1)
    gains = jnp.concatenate([mng_ref[...], gng_ref[...]], axis=1)
    mix = (normed * gains * gate_s[...]).astype(BF16)
    o_ref[0] = x + _dot(mix, wout_ref[...])


def _mixer(x, g, w_cat, sbias, conv_w, conv_b, wa2_pad, ba, mng, gng, w_out, *, ts, mw, gwk, gwv):
    B, S, D = x.shape
    ncol = w_cat.shape[1]
    hd = mw // MLSTM_HEADS
    gdv = gwv // GLA_HEADS
    assert gdv == hd and gwk % LANES == 0 and S % ts == 0 and ts % (2 * CHUNK) == 0
    const = lambda b, s: (0, 0)
    single = pl.Buffered(1)
    kern = functools.partial(_mixer_kernel, ts=ts, mw=mw, gwk=gwk, gwv=gwv)
    return pl.pallas_call(
        kern,
        grid=(B, S // ts),
        in_specs=[
            pl.BlockSpec((1, ts, D), lambda b, s: (b, s, 0)),
            pl.BlockSpec((1, D), const),
            pl.BlockSpec((D, ncol), const, pipeline_mode=single),
            pl.BlockSpec((1, LANES), const),
            pl.BlockSpec((CONV_W, 2 * mw), const),
            pl.BlockSpec((1, 2 * mw), const),
            pl.BlockSpec((LANES, gwk), const),
            pl.BlockSpec((1, gwk), const),
            pl.BlockSpec((1, mw), const),
            pl.BlockSpec((1, gwv), const),
            pl.BlockSpec((mw + gwv, D), const, pipeline_mode=single),
        ],
        out_specs=pl.BlockSpec((1, ts, D), lambda b, s: (b, s, 0)),
        out_shape=jax.ShapeDtypeStruct((B, S, D), F32),
        scratch_shapes=[
            pltpu.VMEM((ts + 8, 2 * mw), F32),
            pltpu.VMEM((ts, mw), BF16),
            pltpu.VMEM((ts, mw), F32),
            pltpu.VMEM((ts, mw), BF16),
            pltpu.VMEM((ts, mw + gwv), F32),
            pltpu.VMEM((ts, gwk), F32),
            pltpu.VMEM((ts, gwk), F32),
            pltpu.VMEM((ts, gwv), BF16),
            pltpu.VMEM((ts, LANES), F32),
            pltpu.VMEM((ts + CHUNK, LANES), F32),
            pltpu.VMEM((ts, gwk), F32),
            pltpu.VMEM((ts, mw + gwv), F32),
            pltpu.VMEM((MLSTM_HEADS, hd, hd), F32),
            pltpu.VMEM((8, hd), F32),
            pltpu.VMEM((8, LANES), F32),
            pltpu.VMEM((gdv, gwk), F32),
        ],
        compiler_params=pltpu.CompilerParams(
            dimension_semantics=("arbitrary", "arbitrary"), vmem_limit_bytes=VMEM_LIMIT_BYTES),
        name="mixer",
    )(x, g, w_cat, sbias, conv_w, conv_b, wa2_pad, ba, mng, gng, w_out)


def _attn_mlp_kernel(x_ref, k_ref, v_ref, xg_ref, wq_ref, wo_ref, mg_ref, w1_ref, w2_ref, fg_ref,
                     o_ref, *, ff_chunk):
    x = x_ref[...]
    D = x.shape[1]
    hd = D // XATTN_HEADS
    hq = _rmsnorm(x, xg_ref[...]).astype(BF16)
    q = (_dot(hq, wq_ref[...]) * (hd ** -0.5)).astype(BF16)
    atts = []
    for hh in range(XATTN_HEADS):
        cols = slice(hh * hd, (hh + 1) * hd)
        s = _dot_nt(q[:, cols], k_ref[0, :, cols])
        p = jnp.exp(s - jnp.max(s, axis=-1, keepdims=True))
        att = _dot(p.astype(BF16), v_ref[0, :, cols])
        atts.append((att / jnp.sum(p, axis=-1, keepdims=True)).astype(BF16))
    x = x + _dot(jnp.concatenate(atts, axis=1), wo_ref[...])

    hm = _rmsnorm(x, mg_ref[...]).astype(BF16)
    acc = x
    for c in range(w1_ref.shape[1] // ff_chunk):
        cols = slice(c * ff_chunk, (c + 1) * ff_chunk)
        t = jnp.maximum(_dot(hm, w1_ref[:, cols]), 0.0)
        acc = acc + _dot((t * t).astype(BF16), w2_ref[cols, :])
    o_ref[...] = _rmsnorm(acc, fg_ref[...])


def _attn_mlp(x2d, k, v, xg, wq, wo, mg, w1, w2, fg, *, tm, seq, ff_chunk):
    T, D = x2d.shape
    M = k.shape[1]
    F = w1.shape[1]
    assert T % tm == 0 and seq % tm == 0 and F % ff_chunk == 0
    per_b = seq // tm
    const = lambda i: (0, 0)
    single = pl.Buffered(1)
    kern = functools.partial(_attn_mlp_kernel, ff_chunk=ff_chunk)
    return pl.pallas_call(
        kern,
        grid=(T // tm,),
        in_specs=[
            pl.BlockSpec((tm, D), lambda i: (i, 0)),
            pl.BlockSpec((1, M, D), lambda i: (i // per_b, 0, 0)),
            pl.BlockSpec((1, M, D), lambda i: (i // per_b, 0, 0)),
            pl.BlockSpec((1, D), const),
            pl.BlockSpec((D, D), const, pipeline_mode=single),
            pl.BlockSpec((D, D), const, pipeline_mode=single),
            pl.BlockSpec((1, D), const),
            pl.BlockSpec((D, F), const, pipeline_mode=single),
            pl.BlockSpec((F, D), const, pipeline_mode=single),
            pl.BlockSpec((1, D), const),
        ],
        out_specs=pl.BlockSpec((tm, D), lambda i: (i, 0)),
        out_shape=jax.ShapeDtypeStruct((T, D), F32),
        compiler_params=pltpu.CompilerParams(
            dimension_semantics=("arbitrary",), vmem_limit_bytes=VMEM_LIMIT_BYTES),
        name="attn_mlp",
    )(x2d, k, v, xg, wq, wo, mg, w1, w2, fg)


def _layer(x, mem, mix_norm_g, w_in, conv_w, conv_b, i_b, f_b, mlstm_norm_g, wa2, ba, gla_norm_g,
           w_out, xattn_norm_g, mem_norm_g, wq, wk, wv, wo, mlp_norm_g, w1, w2, final_g, *, final):
    B, S, D = x.shape
    mw = mlstm_norm_g.shape[0]
    gwv = gla_norm_g.shape[0]
    gwk = wa2.shape[1]
    rank = wa2.shape[0]
    nh = i_b.shape[0]
    assert nh == MLSTM_HEADS and rank == GLA_RANK

    o_mi = 4 * mw
    o_mf = o_mi + nh
    o_gq = o_mf + nh
    o_ga = o_gq + 2 * gwk + 2 * gwv
    small_w = jnp.concatenate(
        [w_in[:, o_mi:o_gq], w_in[:, o_ga:o_ga + rank],
         jnp.zeros((D, LANES - 2 * nh - rank), w_in.dtype)], axis=1)
    w_cat = jnp.concatenate([w_in[:, 0:o_mi], w_in[:, o_gq:o_ga], small_w], axis=1).astype(BF16)
    sbias = jnp.concatenate([i_b, f_b, jnp.zeros((LANES - 2 * nh,), F32)])[None, :]
    wa2_pad = jnp.zeros((LANES, gwk), F32).at[2 * nh:2 * nh + rank].set(wa2).astype(BF16)

    k_mem, v_mem = _mem_kv(mem, mem_norm_g[None, :], wk.astype(BF16), wv.astype(BF16))
    x = _mixer(x, mix_norm_g[None, :], w_cat, sbias, conv_w, conv_b[None, :], wa2_pad, ba[None, :],
               mlstm_norm_g[None, :], gla_norm_g[None, :], w_out.astype(BF16),
               ts=512, mw=mw, gwk=gwk, gwv=gwv)
    assert final
    out = _attn_mlp(x.reshape(B * S, D), k_mem, v_mem, xattn_norm_g[None, :], wq.astype(BF16),
                    wo.astype(BF16), mlp_norm_g[None, :], w1.astype(BF16), w2.astype(BF16),
                    final_g[None, :], tm=512, seq=S, ff_chunk=1024)
    return out.reshape(B, S, D)


def kernel(x, mem, mix_norm_g, w_in, conv_w, conv_b, mlstm_i_b, mlstm_f_b, mlstm_norm_g, gla_wa2, gla_ba,
           gla_norm_g, w_out, xattn_norm_g, mem_norm_g, wq_x, wk_x, wv_x, wo_x, mlp_norm_g, w1, w2,
           final_norm_g):
    depth = w_in.shape[0]
    assert depth == 1
    return _layer(x, mem, mix_norm_g[0], w_in[0], conv_w[0], conv_b[0], mlstm_i_b[0], mlstm_f_b[0],
                  mlstm_norm_g[0], gla_wa2[0], gla_ba[0], gla_norm_g[0], w_out[0], xattn_norm_g[0],
                  mem_norm_g[0], wq_x[0], wk_x[0], wv_x[0], wo_x[0], mlp_norm_g[0], w1[0], w2[0],
                  final_norm_g, final=True)
```

```python
import functools

import jax
import jax.numpy as jnp
from jax import lax
from jax.experimental import pallas as pl
from jax.experimental.pallas import tpu as pltpu

EPS = 1e-6
CHUNK = 64
CONV_W = 4
MLSTM_HEADS = 4
GLA_HEADS = 4
GLA_RANK = 16
GLA_TAU = 16.0
XATTN_HEADS = 4
LANES = 128
VMEM_LIMIT_BYTES = 56 * 1024 * 1024

BF16 = jnp.bfloat16
F32 = jnp.float32


def _rmsnorm(x, g):
    return x * lax.rsqrt(jnp.mean(x * x, axis=-1, keepdims=True) + EPS) * g


def _log_sigmoid(z):
    return jnp.minimum(z, 0.0) - jnp.log1p(jnp.exp(-jnp.abs(z)))


def _dot(a, b):
    return jnp.dot(a, b, preferred_element_type=F32)


def _dot_nt(a, b):
    return lax.dot_general(a, b, (((1,), (1,)), ((), ())), preferred_element_type=F32)


def _dot_tn(a, b):
    return lax.dot_general(a, b, (((0,), (0,)), ((), ())), preferred_element_type=F32)


def _segment_scan(x, seg, op, identity):
    row = lax.broadcasted_iota(jnp.int32, x.shape, 0) & (seg - 1)
    k = 1
    while k < seg:
        shifted = pltpu.roll(x, k, axis=0)
        x = op(x, jnp.where(row >= k, shifted, identity))
        k *= 2
    return x


def _mem_kv_kernel(mem_ref, g_ref, wk_ref, wv_ref, k_ref, v_ref):
    mn = _rmsnorm(mem_ref[0], g_ref[...]).astype(BF16)
    k_ref[0] = _dot(mn, wk_ref[...]).astype(BF16)
    v_ref[0] = _dot(mn, wv_ref[...]).astype(BF16)


def _mem_kv(mem, g, wk, wv):
    B, M, D = mem.shape
    const = lambda b: (0, 0)
    return pl.pallas_call(
        _mem_kv_kernel,
        grid=(B,),
        in_specs=[
            pl.BlockSpec((1, M, D), lambda b: (b, 0, 0)),
            pl.BlockSpec((1, D), const),
            pl.BlockSpec((D, D), const),
            pl.BlockSpec((D, D), const),
        ],
        out_specs=[
            pl.BlockSpec((1, M, D), lambda b: (b, 0, 0)),
            pl.BlockSpec((1, M, D), lambda b: (b, 0, 0)),
        ],
        out_shape=[jax.ShapeDtypeStruct((B, M, D), BF16)] * 2,
        compiler_params=pltpu.CompilerParams(
            dimension_semantics=("arbitrary",), vmem_limit_bytes=VMEM_LIMIT_BYTES),
        name="mem_kv",
    )(mem, g, wk, wv)


def _mixer_kernel(x_ref, g_ref, w_ref, sbias_ref, convw_ref, convb_ref, wa2_ref, ba_ref,
                  mng_ref, gng_ref, wout_ref,
                  o_ref,
                  cbuf, q_s, k_s, v1_s, gate_s, gq_s, gk_s, gv_s, bg_s, hraw_s,
                  b_s, cm_s, at_s, wk_s, wloc_s, wint_s, emt_s, dec_s, su_s,
                  c_s, m_s, st_s, *, ts, mw, gwk, gwv):
    nc = ts // CHUNK
    hd = mw // MLSTM_HEADS
    gdk = gwk // GLA_HEADS
    gdv = gwv // GLA_HEADS
    o_v = 2 * mw
    o_o = 3 * mw
    o_gq = 4 * mw
    o_gk = o_gq + gwk
    o_gv = o_gk + gwk
    o_gr = o_gv + gwv
    o_sm = o_gr + gwv

    @pl.when(pl.program_id(1) == 0)
    def _():
        cbuf[0:8, :] = jnp.zeros((8, cbuf.shape[1]), F32)
        c_s[...] = jnp.zeros(c_s.shape, F32)
        m_s[...] = jnp.zeros(m_s.shape, F32)
        st_s[...] = jnp.zeros(st_s.shape, F32)
        for hh in range(MLSTM_HEADS):
            v1_s[:, (2 * hh + 1) * hd:(2 * hh + 2) * hd] = jnp.ones((ts, hd), BF16)

    x = x_ref[0]
    h = _rmsnorm(x, g_ref[...]).astype(BF16)

    cbuf[pl.ds(8, ts), :] = _dot(h, w_ref[:, 0:o_v])
    v = _dot(h, w_ref[:, o_v:o_o]).astype(BF16)
    for hh in range(MLSTM_HEADS):
        v1_s[:, 2 * hh * hd:(2 * hh + 1) * hd] = v[:, hh * hd:(hh + 1) * hd]
    gate_s[:, 0:mw] = jax.nn.sigmoid(_dot(h, w_ref[:, o_o:o_gq]))
    gq_s[...] = _dot(h, w_ref[:, o_gq:o_gk])
    gk_s[...] = _dot(h, w_ref[:, o_gk:o_gv])
    gv_s[...] = _dot(h, w_ref[:, o_gv:o_gr]).astype(BF16)
    gr = _dot(h, w_ref[:, o_gr:o_sm])
    gate_s[:, mw:mw + gwv] = gr * jax.nn.sigmoid(gr)
    small = _dot(h, w_ref[:, o_sm:o_sm + LANES]) + sbias_ref[...]

    y = convb_ref[...] + cbuf[pl.ds(8, ts), :] * convw_ref[3:4, :]
    y = y + cbuf[pl.ds(7, ts), :] * convw_ref[2:3, :]
    y = y + cbuf[pl.ds(6, ts), :] * convw_ref[1:2, :]
    y = y + cbuf[pl.ds(5, ts), :] * convw_ref[0:1, :]
    tail = cbuf[pl.ds(ts, 8), :]
    cbuf[0:8, :] = tail
    y = y * jax.nn.sigmoid(y)
    q_s[...] = (y[:, 0:mw] * (hd ** -0.5)).astype(BF16)
    k_s[...] = y[:, mw:2 * mw]

    b = pltpu.roll(_segment_scan(_log_sigmoid(small), CHUNK, jnp.add, 0.0),
                   LANES - MLSTM_HEADS, axis=1)
    a = small - b
    cm = _segment_scan(a, CHUNK, jnp.maximum, -jnp.inf)
    b_s[...] = b
    cm_s[...] = cm
    b_last = b_s[pl.ds(CHUNK - 1, nc, stride=CHUNK), :]
    cm_last = cm_s[pl.ds(CHUNK - 1, nc, stride=CHUNK), :]
    m = m_s[0:1, :]
    m_rows = []
    cml_rows = []
    for c in range(nc):
        bl = b_last[c:c + 1, :]
        gmax = bl + cm_last[c:c + 1, :]
        m_new = jnp.maximum(bl + m, gmax)
        dec_s[c:c + 1, :] = jnp.exp(bl + m - m_new)
        su_s[c:c + 1, :] = jnp.exp(gmax - m_new)
        m_rows.append(jnp.broadcast_to(m, (CHUNK, LANES)))
        cml_rows.append(jnp.broadcast_to(cm_last[c:c + 1, :], (CHUNK, LANES)))
        m = m_new
    m_s[0:1, :] = m
    mc = jnp.concatenate(m_rows, axis=0)
    mx = jnp.maximum(cm, mc)
    wloc_s[...] = jnp.exp(cm - mx)
    wint_s[...] = jnp.exp(mc - mx)
    emt_s[...] = jnp.exp(-(b + mx))
    wk_s[...] = jnp.exp(a - jnp.concatenate(cml_rows, axis=0))
    at_s[...] = a.T

    la = _log_sigmoid(_dot(small.astype(BF16), wa2_ref[...]) + ba_ref[...]) * (1.0 / GLA_TAU)
    bg_s[...] = _segment_scan(la, CHUNK, jnp.add, 0.0)

    ti = lax.broadcasted_iota(jnp.int32, (CHUNK, CHUNK), 0)
    si = lax.broadcasted_iota(jnp.int32, (CHUNK, CHUNK), 1)
    causal = ti >= si
    lane = lax.broadcasted_iota(jnp.int32, (CHUNK, gwk), 1)

    def chunk(c, r0, a_rows):
        rows = pl.ds(r0, CHUNK)
        cmc = cm_s[rows, :]
        wkc = wk_s[rows, :]
        wl = wloc_s[rows, :]
        wi = wint_s[rows, :]
        em = emt_s[rows, :]
        dec = dec_s[pl.ds(c, 1), :]
        su = su_s[pl.ds(c, 1), :]
        for hh in range(MLSTM_HEADS):
            cols = slice(hh * hd, (hh + 1) * hd)
            col = slice(hh, hh + 1)
            qh = q_s[rows, cols]
            kh = k_s[rows, cols]
            v1h = v1_s[rows, 2 * hh * hd:(2 * hh + 2) * hd]
            c_prev = c_s[hh]
            wk = (wkc[:, col] * kh).astype(BF16)
            c_s[hh] = dec[:, col] * c_prev + su[:, col] * _dot_tn(wk, v1h)

            pm = jnp.exp(jnp.where(causal, a_rows[hh:hh + 1, :] - cmc[:, col], -jnp.inf))
            s = _dot_nt(qh, kh.astype(BF16)) * pm
            r = (wl[:, col] * _dot(s.astype(BF16), v1h)
                 + wi[:, col] * _dot(qh, c_prev.astype(BF16)))
            hraw_s[rows, cols] = r[:, 0:hd] / jnp.maximum(jnp.abs(r[:, hd:2 * hd]), em[:, col])

        bg = bg_s[rows, :]
        bg_last = bg[CHUNK - 1:CHUNK, :]
        gq = gq_s[rows, :]
        gk = gk_s[rows, :]
        gv = gv_s[rows, :]
        qe = gq * jnp.exp(bg) * (gdk ** -0.5)
        ke = (gk * jnp.exp(-bg)).astype(BF16)
        kw = gk * jnp.exp(bg_last - bg)
        st_prev = st_s[...]
        qe_heads = []
        kw_heads = []
        for hh in range(GLA_HEADS):
            hm = (lane >= hh * gdk) & (lane < (hh + 1) * gdk)
            qe_heads.append(jnp.where(hm, qe, 0.0).astype(BF16))
            kw_heads.append(jnp.where(hm, kw, 0.0).astype(BF16))
        qe_stack = jnp.concatenate(qe_heads, axis=0)
        kw_stack = jnp.concatenate(kw_heads, axis=0)
        v_stack = jnp.concatenate(
            [gv[:, hh * gdv:(hh + 1) * gdv] for hh in range(GLA_HEADS)], axis=0)
        a_all = _dot_nt(qe_stack, ke)
        inter_all = _dot_nt(qe_stack, st_prev.astype(BF16))
        for hh in range(GLA_HEADS):
            hrows = slice(hh * CHUNK, (hh + 1) * CHUNK)
            a_h = jnp.where(causal, a_all[hrows, :], 0.0).astype(BF16)
            hraw_s[rows, mw + hh * gdv:mw + (hh + 1) * gdv] = (
                _dot(a_h, gv[:, hh * gdv:(hh + 1) * gdv]) + inter_all[hrows, :])
        st_s[...] = st_prev * jnp.exp(bg_last) + _dot_tn(v_stack, kw_stack)

    def pair_body(p, carry):
        r0 = pl.multiple_of(p * 2 * CHUNK, 2 * CHUNK)
        at2 = at_s[0:8, pl.ds(r0, 2 * CHUNK)]
        chunk(2 * p, r0, at2[:, 0:CHUNK])
        chunk(2 * p + 1, pl.multiple_of(r0 + CHUNK, CHUNK), at2[:, CHUNK:2 * CHUNK])
        return carry

    lax.fori_loop(0, nc // 2, pair_body, 0, unroll=True)

    parts = []
    for hh in range(MLSTM_HEADS + GLA_HEADS):
        hv = hraw_s[:, hh * hd:(hh + 1) * hd]
        parts.append(hv * lax.rsqrt(jnp.mean(hv * hv, axis=-1, keepdims=True) + EPS))
    normed = jnp.concatenate(parts, axis=1)
    gains = jnp.concatenate([mng_ref[...], gng_ref[...]], axis=1)
    mix = (normed * gains * gate_s[...]).astype(BF16)
    o_ref[0] = x + _dot(mix, wout_ref[...])


def _mixer(x, g, w_cat, sbias, conv_w, conv_b, wa2_pad, ba, mng, gng, w_out, *, ts, mw, gwk, gwv):
    B, S, D = x.shape
    ncol = w_cat.shape[1]
    nc = ts // CHUNK
    hd = mw // MLSTM_HEADS
    gdv = gwv // GLA_HEADS
    assert gdv == hd and hd == LANES and gwk % LANES == 0 and S % ts == 0 and ts % (2 * CHUNK) == 0
    const = lambda b, s: (0, 0)
    single = pl.Buffered(1)
    col_f32 = pltpu.VMEM((ts, LANES), F32)
    kern = functools.partial(_mixer_kernel, ts=ts, mw=mw, gwk=gwk, gwv=gwv)
    return pl.pallas_call(
        kern,
        grid=(B, S // ts),
        in_specs=[
            pl.BlockSpec((1, ts, D), lambda b, s: (b, s, 0)),
            pl.BlockSpec((1, D), const),
            pl.BlockSpec((D, ncol), const, pipeline_mode=single),
            pl.BlockSpec((1, LANES), const),
            pl.BlockSpec((CONV_W, 2 * mw), const),
            pl.BlockSpec((1, 2 * mw), const),
            pl.BlockSpec((LANES, gwk), const),
            pl.BlockSpec((1, gwk), const),
            pl.BlockSpec((1, mw), const),
            pl.BlockSpec((1, gwv), const),
            pl.BlockSpec((mw + gwv, D), const, pipeline_mode=single),
        ],
        out_specs=pl.BlockSpec((1, ts, D), lambda b, s: (b, s, 0)),
        out_shape=jax.ShapeDtypeStruct((B, S, D), F32),
        scratch_shapes=[
            pltpu.VMEM((ts + 8, 2 * mw), F32),
            pltpu.VMEM((ts, mw), BF16),
            pltpu.VMEM((ts, mw), F32),
            pltpu.VMEM((ts, 2 * mw), BF16),
            pltpu.VMEM((ts, mw + gwv), F32),
            pltpu.VMEM((ts, gwk), F32),
            pltpu.VMEM((ts, gwk), F32),
            pltpu.VMEM((ts, gwv), BF16),
            pltpu.VMEM((ts, gwk), F32),
            pltpu.VMEM((ts, mw + gwv), F32),
            col_f32,
            col_f32,
            pltpu.VMEM((LANES, ts), F32),
            col_f32,
            col_f32,
            col_f32,
            col_f32,
            pltpu.VMEM((max(nc, 8), LANES), F32),
            pltpu.VMEM((max(nc, 8), LANES), F32),
            pltpu.VMEM((MLSTM_HEADS, hd, 2 * hd), F32),
            pltpu.VMEM((8, LANES), F32),
            pltpu.VMEM((gdv, gwk), F32),
        ],
        compiler_params=pltpu.CompilerParams(
            dimension_semantics=("arbitrary", "arbitrary"), vmem_limit_bytes=VMEM_LIMIT_BYTES),
        name="mixer",
    )(x, g, w_cat, sbias, conv_w, conv_b, wa2_pad, ba, mng, gng, w_out)


def _attn_mlp_kernel(x_ref, k_ref, v_ref, xg_ref, wq_ref, wo_ref, mg_ref, w1_ref, w2_ref, fg_ref,
                     o_ref, *, ff_chunk):
    x = x_ref[...]
    D = x.shape[1]
    hd = D // XATTN_HEADS
    hq = _rmsnorm(x, xg_ref[...]).astype(BF16)
    q = (_dot(hq, wq_ref[...]) * (hd ** -0.5)).astype(BF16)
    atts = []
    for hh in range(XATTN_HEADS):
        cols = slice(hh * hd, (hh + 1) * hd)
        s = _dot_nt(q[:, cols], k_ref[0, :, cols])
        p = jnp.exp(s - jnp.max(s, axis=-1, keepdims=True))
        att = _dot(p.astype(BF16), v_ref[0, :, cols])
        atts.append((att / jnp.sum(p, axis=-1, keepdims=True)).astype(BF16))
    x = x + _dot(jnp.concatenate(atts, axis=1), wo_ref[...])

    hm = _rmsnorm(x, mg_ref[...]).astype(BF16)
    acc = x
    for c in range(w1_ref.shape[1] // ff_chunk):
        cols = slice(c * ff_chunk, (c + 1) * ff_chunk)
        t = jnp.maximum(_dot(hm, w1_ref[:, cols]), 0.0)
        acc = acc + _dot((t * t).astype(BF16), w2_ref[cols, :])
    o_ref[...] = _rmsnorm(acc, fg_ref[...])


def _attn_mlp(x2d, k, v, xg, wq, wo, mg, w1, w2, fg, *, tm, seq, ff_chunk):
    T, D = x2d.shape
    M = k.shape[1]
    F = w1.shape[1]
    assert T % tm == 0 and seq % tm == 0 and F % ff_chunk == 0
    per_b = seq // tm
    const = lambda i: (0, 0)
    single = pl.Buffered(1)
    kern = functools.partial(_attn_mlp_kernel, ff_chunk=ff_chunk)
    return pl.pallas_call(
        kern,
        grid=(T // tm,),
        in_specs=[
            pl.BlockSpec((tm, D), lambda i: (i, 0)),
            pl.BlockSpec((1, M, D), lambda i: (i // per_b, 0, 0)),
            pl.BlockSpec((1, M, D), lambda i: (i // per_b, 0, 0)),
            pl.BlockSpec((1, D), const),
            pl.BlockSpec((D, D), const, pipeline_mode=single),
            pl.BlockSpec((D, D), const, pipeline_mode=single),
            pl.BlockSpec((1, D), const),
            pl.BlockSpec((D, F), const, pipeline_mode=single),
            pl.BlockSpec((F, D), const, pipeline_mode=single),
            pl.BlockSpec((1, D), const),
        ],
        out_specs=pl.BlockSpec((tm, D), lambda i: (i, 0)),
        out_shape=jax.ShapeDtypeStruct((T, D), F32),
        compiler_params=pltpu.CompilerParams(
            dimension_semantics=("arbitrary",), vmem_limit_bytes=VMEM_LIMIT_BYTES),
        name="attn_mlp",
    )(x2d, k, v, xg, wq, wo, mg, w1, w2, fg)


MIXER_SEQ_TILE = 512
ATTN_MLP_TILE = 512
FF_CHUNK = 1024


def kernel(x, mem, mix_norm_g, w_in, conv_w, conv_b, mlstm_i_b, mlstm_f_b, mlstm_norm_g, gla_wa2, gla_ba,
           gla_norm_g, w_out, xattn_norm_g, mem_norm_g, wq_x, wk_x, wv_x, wo_x, mlp_norm_g, w1, w2,
           final_norm_g):
    assert w_in.shape[0] == 1, "single-layer block"
    B, S, D = x.shape
    mw = mlstm_norm_g.shape[1]
    gwv = gla_norm_g.shape[1]
    rank, gwk = gla_wa2.shape[1:]
    nh = mlstm_i_b.shape[1]
    assert nh == MLSTM_HEADS and rank == GLA_RANK
    w_in = w_in[0]

    o_mi = 4 * mw
    o_gq = o_mi + 2 * nh
    o_ga = o_gq + 2 * gwk + 2 * gwv
    small_w = jnp.concatenate(
        [w_in[:, o_mi:o_gq], w_in[:, o_ga:o_ga + rank],
         jnp.zeros((D, LANES - 2 * nh - rank), w_in.dtype)], axis=1)
    w_cat = jnp.concatenate([w_in[:, 0:o_mi], w_in[:, o_gq:o_ga], small_w], axis=1).astype(BF16)
    sbias = jnp.concatenate([mlstm_i_b[0], mlstm_f_b[0], jnp.zeros((LANES - 2 * nh,), F32)])[None, :]
    wa2_pad = jnp.zeros((LANES, gwk), F32).at[2 * nh:2 * nh + rank].set(gla_wa2[0]).astype(BF16)

    k_mem, v_mem = _mem_kv(mem, mem_norm_g, wk_x[0].astype(BF16), wv_x[0].astype(BF16))
    x = _mixer(x, mix_norm_g, w_cat, sbias, conv_w[0], conv_b, wa2_pad, gla_ba,
               mlstm_norm_g, gla_norm_g, w_out[0].astype(BF16),
               ts=MIXER_SEQ_TILE, mw=mw, gwk=gwk, gwv=gwv)
    out = _attn_mlp(x.reshape(B * S, D), k_mem, v_mem, xattn_norm_g, wq_x[0].astype(BF16),
                    wo_x[0].astype(BF16), mlp_norm_g, w1[0].astype(BF16), w2[0].astype(BF16),
                    final_norm_g[None, :], tm=ATTN_MLP_TILE, seq=S, ff_chunk=FF_CHUNK)
    return out.reshape(B, S, D)
```

```python
import functools

import jax
import jax.numpy as jnp
from jax import lax
from jax.experimental import pallas as pl
from jax.experimental.pallas import tpu as pltpu

EPS = 1e-6
CHUNK = 64
CONV_W = 4
MLSTM_HEADS = 4
GLA_HEADS = 4
GLA_RANK = 16
GLA_TAU = 16.0
XATTN_HEADS = 4
LANES = 128
VMEM_LIMIT_BYTES = 56 * 1024 * 1024

BF16 = jnp.bfloat16
F32 = jnp.float32


def _rmsnorm(x, g):
    return x * lax.rsqrt(jnp.mean(x * x, axis=-1, keepdims=True) + EPS) * g


def _log_sigmoid(z):
    return jnp.minimum(z, 0.0) - jnp.log(1.0 + jnp.exp(-jnp.abs(z)))


def _sigmoid(z):
    return 0.5 * jnp.tanh(0.5 * z) + 0.5


def _dot(a, b):
    return jnp.dot(a, b, preferred_element_type=F32)


def _dot_nt(a, b):
    return lax.dot_general(a, b, (((1,), (1,)), ((), ())), preferred_element_type=F32)


def _dot_tn(a, b):
    return lax.dot_general(a, b, (((0,), (0,)), ((), ())), preferred_element_type=F32)


def _segment_scan(x, seg, op, identity):
    row = lax.broadcasted_iota(jnp.int32, x.shape, 0) & (seg - 1)
    k = 1
    while k < seg:
        shifted = pltpu.roll(x, k, axis=0)
        x = op(x, jnp.where(row >= k, shifted, identity))
        k *= 2
    return x


def _mem_kv_kernel(mem_ref, g_ref, wk_ref, wv_ref, k_ref, v_ref):
    mn = _rmsnorm(mem_ref[0], g_ref[...]).astype(BF16)
    k_ref[0] = _dot(mn, wk_ref[...]).astype(BF16)
    v_ref[0] = _dot(mn, wv_ref[...]).astype(BF16)


def _mem_kv(mem, g, wk, wv):
    B, M, D = mem.shape
    const = lambda b: (0, 0)
    return pl.pallas_call(
        _mem_kv_kernel,
        grid=(B,),
        in_specs=[
            pl.BlockSpec((1, M, D), lambda b: (b, 0, 0)),
            pl.BlockSpec((1, D), const),
            pl.BlockSpec((D, D), const),
            pl.BlockSpec((D, D), const),
        ],
        out_specs=[
            pl.BlockSpec((1, M, D), lambda b: (b, 0, 0)),
            pl.BlockSpec((1, M, D), lambda b: (b, 0, 0)),
        ],
        out_shape=[jax.ShapeDtypeStruct((B, M, D), BF16)] * 2,
        compiler_params=pltpu.CompilerParams(
            dimension_semantics=("arbitrary",), vmem_limit_bytes=VMEM_LIMIT_BYTES),
        name="mem_kv",
    )(mem, g, wk, wv)


N_RAW = 8


def _mixer_kernel(xa_ref, xb_ref, g_ref, w_ref, sbias_ref, convw_ref, convb_ref, wa2_ref, ba_ref,
                  mng_ref, gng_ref, wout_ref,
                  o_ref,
                  *scratch, ts, nt, mw, gwk, gwv):
    raw_sets = (scratch[0:N_RAW], scratch[N_RAW:2 * N_RAW])
    (h_s, carry_s, q_s, qw_s, k_s, kw_s, gate_s, bg_s, hraw_s, b_s, cm_s, at_s, mx_s, emt_s,
     dec_s, c_s, m_s, st_s) = scratch[2 * N_RAW:]
    nc = ts // CHUNK
    hd = mw // MLSTM_HEADS
    gdk = gwk // GLA_HEADS
    gdv = gwv // GLA_HEADS
    o_v = 2 * mw
    o_o = 3 * mw
    o_gq = 4 * mw
    o_gk = o_gq + gwk
    o_gv = o_gk + gwk
    o_gr = o_gv + gwv
    o_sm = o_gr + gwv

    def stage_a_pieces(x_ref, raw):
        uqk, v1, uo, ugq, ugk, ugv, ugr, usm = raw

        def norm():
            h_s[...] = _rmsnorm(x_ref[0], g_ref[...]).astype(BF16)

        def proj(lo, hi):
            return _dot(h_s[...], w_ref[:, lo:hi])

        def p_q():
            uqk[pl.ds(8, ts), 0:mw] = proj(0, mw)

        def p_k():
            uqk[pl.ds(8, ts), mw:2 * mw] = proj(mw, o_v)

        def p_v():
            v = proj(o_v, o_o).astype(BF16)
            for hh in range(MLSTM_HEADS):
                v1[:, 2 * hh * hd:(2 * hh + 1) * hd] = v[:, hh * hd:(hh + 1) * hd]

        def p_o():
            uo[...] = proj(o_o, o_gq)

        def p_gqk():
            u = proj(o_gq, o_gv)
            ugq[...] = u[:, 0:gwk]
            ugk[...] = u[:, gwk:2 * gwk]

        def p_gv():
            ugv[...] = proj(o_gv, o_gr).astype(BF16)

        def p_gr():
            ugr[...] = proj(o_gr, o_sm)

        def p_sm():
            usm[...] = proj(o_sm, o_sm + LANES)

        return [norm, p_q, p_k, p_v, p_o, p_gqk, p_gv, p_gr, p_sm]

    def stage_b(x_ref, raw, a_pieces):
        a_pieces = list(a_pieces)
        for _ in range(min(3, len(a_pieces))):
            a_pieces.pop(0)()
        uqk, v1_s, uo, gq_s, gk_s, gv_s, ugr, usm = raw
        gate_s[:, 0:mw] = _sigmoid(uo[...])
        gr = ugr[...]
        gate_s[:, mw:mw + gwv] = gr * _sigmoid(gr)
        small = usm[...] + sbias_ref[...]

        b = pltpu.roll(_segment_scan(_log_sigmoid(small), CHUNK, jnp.add, 0.0),
                       LANES - MLSTM_HEADS, axis=1)
        a = small - b
        cm = _segment_scan(a, CHUNK, jnp.maximum, -jnp.inf)
        b_s[...] = b
        cm_s[...] = cm
        b_last = b_s[pl.ds(CHUNK - 1, nc, stride=CHUNK), :]
        cm_last = cm_s[pl.ds(CHUNK - 1, nc, stride=CHUNK), :]
        m = m_s[0:1, :]
        m_rows = []
        lw_rows = []
        for c in range(nc):
            bl = b_last[c:c + 1, :]
            gmax = bl + cm_last[c:c + 1, :]
            m_new = jnp.maximum(bl + m, gmax)
            dec_s[c:c + 1, :] = jnp.exp(bl + m - m_new)
            m_rows.append(jnp.broadcast_to(m, (CHUNK, LANES)))
            lw_rows.append(jnp.broadcast_to(gmax - m_new - cm_last[c:c + 1, :], (CHUNK, LANES)))
            m = m_new
        m_s[0:1, :] = m
        mc = jnp.concatenate(m_rows, axis=0)
        mx = jnp.maximum(cm, mc)
        mx_s[...] = mx
        emt_s[...] = jnp.exp(-(b + mx))
        at_s[...] = a.T
        wint = jnp.exp(mc - mx)
        wkey = jnp.exp(a + jnp.concatenate(lw_rows, axis=0))

        uqk[0:8, :] = carry_s[...]
        y = convb_ref[...] + uqk[pl.ds(8, ts), :] * convw_ref[3:4, :]
        y = y + uqk[pl.ds(7, ts), :] * convw_ref[2:3, :]
        y = y + uqk[pl.ds(6, ts), :] * convw_ref[1:2, :]
        y = y + uqk[pl.ds(5, ts), :] * convw_ref[0:1, :]
        carry_s[...] = uqk[pl.ds(ts, 8), :]
        y = y * _sigmoid(y)
        for hh in range(MLSTM_HEADS):
            cols = slice(hh * hd, (hh + 1) * hd)
            qf = y[:, hh * hd:(hh + 1) * hd] * (hd ** -0.5)
            kf = y[:, mw + hh * hd:mw + (hh + 1) * hd]
            q_s[:, cols] = qf.astype(BF16)
            qw_s[:, cols] = (qf * wint[:, hh:hh + 1]).astype(BF16)
            k_s[:, cols] = kf.astype(BF16)
            kw_s[:, cols] = (kf * wkey[:, hh:hh + 1]).astype(BF16)

        la = _log_sigmoid(_dot(small.astype(BF16), wa2_ref[...]) + ba_ref[...]) * (1.0 / GLA_TAU)
        bg_s[...] = _segment_scan(la, CHUNK, jnp.add, 0.0)

        ti = lax.broadcasted_iota(jnp.int32, (CHUNK, CHUNK), 0)
        si = lax.broadcasted_iota(jnp.int32, (CHUNK, CHUNK), 1)
        causal = ti >= si
        lane = lax.broadcasted_iota(jnp.int32, (CHUNK, gwk), 1)

        for c in range(nc):
            rows = pl.ds(c * CHUNK, CHUNK)
            mxc = mx_s[rows, :]
            em = emt_s[rows, :]
            dec = dec_s[c:c + 1, :]
            qk = []
            rhs = []
            for hh in range(MLSTM_HEADS):
                cols = slice(hh * hd, (hh + 1) * hd)
                v1h = v1_s[rows, 2 * hh * hd:(2 * hh + 2) * hd]
                c_prev = c_s[hh]
                qk.append(_dot_nt(q_s[rows, cols], k_s[rows, cols]))
                rhs.append(jnp.concatenate([c_prev.astype(BF16), v1h], axis=0))
                c_s[hh] = dec[:, hh:hh + 1] * c_prev + _dot_tn(kw_s[rows, cols], v1h)
            for hh in range(MLSTM_HEADS):
                cols = slice(hh * hd, (hh + 1) * hd)
                col = slice(hh, hh + 1)
                a_row = at_s[hh:hh + 1, c * CHUNK:(c + 1) * CHUNK]
                pm = jnp.exp(jnp.where(causal, a_row - mxc[:, col], -jnp.inf))
                lhs = jnp.concatenate([qw_s[rows, cols], (qk[hh] * pm).astype(BF16)], axis=1)
                r = _dot(lhs, rhs[hh])
                hraw_s[rows, cols] = r[:, 0:hd] / jnp.maximum(jnp.abs(r[:, hd:2 * hd]), em[:, col])

            bg = bg_s[rows, :]
            bg_last = bg[CHUNK - 1:CHUNK, :]
            gq = gq_s[rows, :]
            gk = gk_s[rows, :]
            gv = gv_s[rows, :]
            qe = gq * jnp.exp(bg) * (gdk ** -0.5)
            ke = (gk * jnp.exp(-bg)).astype(BF16)
            kw = gk * jnp.exp(bg_last - bg)
            st_prev = st_s[...]
            qe_heads = []
            kw_heads = []
            for hh in range(GLA_HEADS):
                hm = (lane >= hh * gdk) & (lane < (hh + 1) * gdk)
                qe_heads.append(jnp.where(hm, qe, 0.0).astype(BF16))
                kw_heads.append(jnp.where(hm, kw, 0.0).astype(BF16))
            qe_stack = jnp.concatenate(qe_heads, axis=0)
            kw_stack = jnp.concatenate(kw_heads, axis=0)
            v_stack = jnp.concatenate(
                [gv[:, hh * gdv:(hh + 1) * gdv] for hh in range(GLA_HEADS)], axis=0)
            a_all = _dot_nt(qe_stack, ke)
            inter_all = _dot_nt(qe_stack, st_prev.astype(BF16))
            st_s[...] = st_prev * jnp.exp(bg_last) + _dot_tn(v_stack, kw_stack)
            for hh in range(GLA_HEADS):
                hrows = slice(hh * CHUNK, (hh + 1) * CHUNK)
                a_h = jnp.where(causal, a_all[hrows, :], 0.0).astype(BF16)
                hraw_s[rows, mw + hh * gdv:mw + (hh + 1) * gdv] = (
                    _dot(a_h, gv[:, hh * gdv:(hh + 1) * gdv]) + inter_all[hrows, :])
            if a_pieces:
                a_pieces.pop(0)()

        while a_pieces:
            a_pieces.pop(0)()

        parts = []
        for hh in range(MLSTM_HEADS + GLA_HEADS):
            hv = hraw_s[:, hh * hd:(hh + 1) * hd]
            parts.append(hv * lax.rsqrt(jnp.mean(hv * hv, axis=-1, keepdims=True) + EPS))
        normed = jnp.concatenate(parts, axis=1)
        gains = jnp.concatenate([mng_ref[...], gng_ref[...]], axis=1)
        mix = (normed * gains * gate_s[...]).astype(BF16)
        o_ref[0] = x_ref[0] + _dot(mix, wout_ref[...])

    j = pl.program_id(0)

    @pl.when(j == 0)
    def _():
        for raw in raw_sets:
            for hh in range(MLSTM_HEADS):
                raw[1][:, (2 * hh + 1) * hd:(2 * hh + 2) * hd] = jnp.ones((ts, hd), BF16)
        for piece in stage_a_pieces(xa_ref, raw_sets[0]):
            piece()

    @pl.when((j > 0) & ((j - 1) % nt == 0))
    def _():
        carry_s[...] = jnp.zeros(carry_s.shape, F32)
        c_s[...] = jnp.zeros(c_s.shape, F32)
        m_s[...] = jnp.zeros(m_s.shape, F32)
        st_s[...] = jnp.zeros(st_s.shape, F32)

    for parity in (0, 1):
        @pl.when((j > 0) & (j % 2 == parity))
        def _():
            stage_b(xb_ref, raw_sets[1 - parity], stage_a_pieces(xa_ref, raw_sets[parity]))


def _mixer(x, g, w_cat, sbias, conv_w, conv_b, wa2_pad, ba, mng, gng, w_out, *, ts, mw, gwk, gwv):
    B, S, D = x.shape
    ncol = w_cat.shape[1]
    nc = ts // CHUNK
    nt = S // ts
    n_tiles = B * nt
    hd = mw // MLSTM_HEADS
    gdv = gwv // GLA_HEADS
    assert gdv == hd and hd == LANES and gwk % LANES == 0 and S % ts == 0 and ts % (2 * CHUNK) == 0
    const = lambda j: (0, 0)

    def tile_a(j):
        t = jnp.minimum(j, n_tiles - 1)
        return (t // nt, t % nt, 0)

    def tile_b(j):
        t = jnp.maximum(j - 1, 0)
        return (t // nt, t % nt, 0)

    single = pl.Buffered(1)
    col_f32 = pltpu.VMEM((ts, LANES), F32)
    raw_set = [
        pltpu.VMEM((ts + 8, 2 * mw), F32),
        pltpu.VMEM((ts, 2 * mw), BF16),
        pltpu.VMEM((ts, mw), F32),
        pltpu.VMEM((ts, gwk), F32),
        pltpu.VMEM((ts, gwk), F32),
        pltpu.VMEM((ts, gwv), BF16),
        pltpu.VMEM((ts, gwv), F32),
        col_f32,
    ]
    assert len(raw_set) == N_RAW
    kern = functools.partial(_mixer_kernel, ts=ts, nt=nt, mw=mw, gwk=gwk, gwv=gwv)
    return pl.pallas_call(
        kern,
        grid=(n_tiles + 1,),
        in_specs=[
            pl.BlockSpec((1, ts, D), tile_a),
            pl.BlockSpec((1, ts, D), tile_b),
            pl.BlockSpec((1, D), const),
            pl.BlockSpec((D, ncol), const, pipeline_mode=single),
            pl.BlockSpec((1, LANES), const),
            pl.BlockSpec((CONV_W, 2 * mw), const),
            pl.BlockSpec((1, 2 * mw), const),
            pl.BlockSpec((LANES, gwk), const),
            pl.BlockSpec((1, gwk), const),
            pl.BlockSpec((1, mw), const),
            pl.BlockSpec((1, gwv), const),
            pl.BlockSpec((mw + gwv, D), const, pipeline_mode=single),
        ],
        out_specs=pl.BlockSpec((1, ts, D), tile_b),
        out_shape=jax.ShapeDtypeStruct((B, S, D), F32),
        scratch_shapes=raw_set + raw_set + [
            pltpu.VMEM((ts, D), BF16),
            pltpu.VMEM((8, 2 * mw), F32),
            pltpu.VMEM((ts, mw), BF16),
            pltpu.VMEM((ts, mw), BF16),
            pltpu.VMEM((ts, mw), BF16),
            pltpu.VMEM((ts, mw), BF16),
            pltpu.VMEM((ts, mw + gwv), F32),
            pltpu.VMEM((ts, gwk), F32),
            pltpu.VMEM((ts, mw + gwv), F32),
            col_f32,
            col_f32,
            pltpu.VMEM((LANES, ts), F32),
            col_f32,
            col_f32,
            pltpu.VMEM((max(nc, 8), LANES), F32),
            pltpu.VMEM((MLSTM_HEADS, hd, 2 * hd), F32),
            pltpu.VMEM((8, LANES), F32),
            pltpu.VMEM((gdv, gwk), F32),
        ],
        compiler_params=pltpu.CompilerParams(
            dimension_semantics=("arbitrary",), vmem_limit_bytes=VMEM_LIMIT_BYTES),
        name="mixer",
    )(x, x, g, w_cat, sbias, conv_w, conv_b, wa2_pad, ba, mng, gng, w_out)


def _attn_mlp_kernel(x_ref, k_ref, v_ref, xg_ref, wq_ref, wo_ref, mg_ref, w1_ref, w2_ref, fg_ref,
                     o_ref, *, ff_chunk):
    x = x_ref[...]
    D = x.shape[1]
    hd = D // XATTN_HEADS
    hq = _rmsnorm(x, xg_ref[...]).astype(BF16)
    q = (_dot(hq, wq_ref[...]) * (hd ** -0.5)).astype(BF16)
    atts = []
    for hh in range(XATTN_HEADS):
        cols = slice(hh * hd, (hh + 1) * hd)
        s = _dot_nt(q[:, cols], k_ref[0, :, cols])
        p = jnp.exp(s - jnp.max(s, axis=-1, keepdims=True))
        att = _dot(p.astype(BF16), v_ref[0, :, cols])
        atts.append((att / jnp.sum(p, axis=-1, keepdims=True)).astype(BF16))
    x = x + _dot(jnp.concatenate(atts, axis=1), wo_ref[...])

    hm = _rmsnorm(x, mg_ref[...]).astype(BF16)
    acc = x
    for c in range(w1_ref.shape[1] // ff_chunk):
        cols = slice(c * ff_chunk, (c + 1) * ff_chunk)
        t = jnp.maximum(_dot(hm, w1_ref[:, cols]), 0.0)
        acc = acc + _dot((t * t).astype(BF16), w2_ref[cols, :])
    o_ref[...] = _rmsnorm(acc, fg_ref[...])


def _attn_mlp(x2d, k, v, xg, wq, wo, mg, w1, w2, fg, *, tm, seq, ff_chunk):
    T, D = x2d.shape
    M = k.shape[1]
    F = w1.shape[1]
    assert T % tm == 0 and seq % tm == 0 and F % ff_chunk == 0
    per_b = seq // tm
    const = lambda i: (0, 0)
    single = pl.Buffered(1)
    kern = functools.partial(_attn_mlp_kernel, ff_chunk=ff_chunk)
    return pl.pallas_call(
        kern,
        grid=(T // tm,),
        in_specs=[
            pl.BlockSpec((tm, D), lambda i: (i, 0)),
            pl.BlockSpec((1, M, D), lambda i: (i // per_b, 0, 0)),
            pl.BlockSpec((1, M, D), lambda i: (i // per_b, 0, 0)),
            pl.BlockSpec((1, D), const),
            pl.BlockSpec((D, D), const, pipeline_mode=single),
            pl.BlockSpec((D, D), const, pipeline_mode=single),
            pl.BlockSpec((1, D), const),
            pl.BlockSpec((D, F), const, pipeline_mode=single),
            pl.BlockSpec((F, D), const, pipeline_mode=single),
            pl.BlockSpec((1, D), const),
        ],
        out_specs=pl.BlockSpec((tm, D), lambda i: (i, 0)),
        out_shape=jax.ShapeDtypeStruct((T, D), F32),
        compiler_params=pltpu.CompilerParams(
            dimension_semantics=("arbitrary",), vmem_limit_bytes=VMEM_LIMIT_BYTES),
        name="attn_mlp",
    )(x2d, k, v, xg, wq, wo, mg, w1, w2, fg)


MIXER_SEQ_TILE = 512
ATTN_MLP_TILE = 512
FF_CHUNK = 1024


def kernel(x, mem, mix_norm_g, w_in, conv_w, conv_b, mlstm_i_b, mlstm_f_b, mlstm_norm_g, gla_wa2, gla_ba,
           gla_norm_g, w_out, xattn_norm_g, mem_norm_g, wq_x, wk_x, wv_x, wo_x, mlp_norm_g, w1, w2,
           final_norm_g):
    assert w_in.shape[0] == 1, "single-layer block"
    B, S, D = x.shape
    mw = mlstm_norm_g.shape[1]
    gwv = gla_norm_g.shape[1]
    rank, gwk = gla_wa2.shape[1:]
    nh = mlstm_i_b.shape[1]
    assert nh == MLSTM_HEADS and rank == GLA_RANK
    w_in = w_in[0]

    o_mi = 4 * mw
    o_gq = o_mi + 2 * nh
    o_ga = o_gq + 2 * gwk + 2 * gwv
    small_w = jnp.concatenate(
        [w_in[:, o_mi:o_gq], w_in[:, o_ga:o_ga + rank],
         jnp.zeros((D, LANES - 2 * nh - rank), w_in.dtype)], axis=1)
    w_cat = jnp.concatenate([w_in[:, 0:o_mi], w_in[:, o_gq:o_ga], small_w], axis=1).astype(BF16)
    sbias = jnp.concatenate([mlstm_i_b[0], mlstm_f_b[0], jnp.zeros((LANES - 2 * nh,), F32)])[None, :]
    wa2_pad = jnp.zeros((LANES, gwk), F32).at[2 * nh:2 * nh + rank].set(gla_wa2[0]).astype(BF16)

    k_mem, v_mem = _mem_kv(mem, mem_norm_g, wk_x[0].astype(BF16), wv_x[0].astype(BF16))
    x = _mixer(x, mix_norm_g, w_cat, sbias, conv_w[0], conv_b, wa2_pad, gla_ba,
               mlstm_norm_g, gla_norm_g, w_out[0].astype(BF16),
               ts=MIXER_SEQ_TILE, mw=mw, gwk=gwk, gwv=gwv)
    out = _attn_mlp(x.reshape(B * S, D), k_mem, v_mem, xattn_norm_g, wq_x[0].astype(BF16),
                    wo_x[0].astype(BF16), mlp_norm_g, w1[0].astype(BF16), w2[0].astype(BF16),
                    final_norm_g[None, :], tm=ATTN_MLP_TILE, seq=S, ff_chunk=FF_CHUNK)
    return out.reshape(B, S, D)
```

```python
import functools

import jax
import jax.numpy as jnp
from jax import lax
from jax.experimental import pallas as pl
from jax.experimental.pallas import tpu as pltpu

EPS = 1e-6
CHUNK = 64
CONV_W = 4
MLSTM_HEADS = 4
GLA_HEADS = 4
GLA_RANK = 16
GLA_TAU = 16.0
XATTN_HEADS = 4
LANES = 128
VMEM_LIMIT_BYTES = 56 * 1024 * 1024

BF16 = jnp.bfloat16
F32 = jnp.float32


def _rmsnorm(x, g):
    return x * lax.rsqrt(jnp.mean(x * x, axis=-1, keepdims=True) + EPS) * g


def _log_sigmoid(z):
    return jnp.minimum(z, 0.0) - jnp.log(1.0 + jnp.exp(-jnp.abs(z)))


def _sigmoid(z):
    return 0.5 * jnp.tanh(0.5 * z) + 0.5


def _dot(a, b):
    return jnp.dot(a, b, preferred_element_type=F32)


def _dot_nt(a, b):
    return lax.dot_general(a, b, (((1,), (1,)), ((), ())), preferred_element_type=F32)


def _dot_tn(a, b):
    return lax.dot_general(a, b, (((0,), (0,)), ((), ())), preferred_element_type=F32)


def _segment_scan(x, seg, op, identity):
    row = lax.broadcasted_iota(jnp.int32, x.shape, 0) & (seg - 1)
    k = 1
    while k < seg:
        shifted = pltpu.roll(x, k, axis=0)
        x = op(x, jnp.where(row >= k, shifted, identity))
        k *= 2
    return x


def _mem_kv_kernel(mem_ref, g_ref, wk_ref, wv_ref, k_ref, v_ref):
    mn = _rmsnorm(mem_ref[0], g_ref[...]).astype(BF16)
    k_ref[0] = _dot(mn, wk_ref[...]).astype(BF16)
    v_ref[0] = _dot(mn, wv_ref[...]).astype(BF16)


def _mem_kv(mem, g, wk, wv):
    B, M, D = mem.shape
    const = lambda b: (0, 0)
    return pl.pallas_call(
        _mem_kv_kernel,
        grid=(B,),
        in_specs=[
            pl.BlockSpec((1, M, D), lambda b: (b, 0, 0)),
            pl.BlockSpec((1, D), const),
            pl.BlockSpec((D, D), const),
            pl.BlockSpec((D, D), const),
        ],
        out_specs=[
            pl.BlockSpec((1, M, D), lambda b: (b, 0, 0)),
            pl.BlockSpec((1, M, D), lambda b: (b, 0, 0)),
        ],
        out_shape=[jax.ShapeDtypeStruct((B, M, D), BF16)] * 2,
        compiler_params=pltpu.CompilerParams(
            dimension_semantics=("arbitrary",), vmem_limit_bytes=VMEM_LIMIT_BYTES),
        name="mem_kv",
    )(mem, g, wk, wv)


N_RAW = 9
GLA_SAFE_LOG_RANGE = 60.0


def _mixer_kernel(xa_ref, xb_ref, g_ref, w_ref, sbias_ref, convw_ref, convb_ref, wa2_ref, ba_ref,
                  mng_ref, gng_ref, wout_ref,
                  o_ref,
                  *scratch, ts, nt, mw, gwk, gwv):
    raw_sets = (scratch[0:N_RAW], scratch[N_RAW:2 * N_RAW])
    (flag_s, gintra_s, h_s, carry_s, q_s, qw_s, k_s, kw_s, gate_s, hraw_s, b_s, cm_s, at_s, mx_s,
     emt_s, dec_s, c_s, m_s, st_s) = scratch[2 * N_RAW:]
    nc = ts // CHUNK
    hd = mw // MLSTM_HEADS
    gdk = gwk // GLA_HEADS
    gdv = gwv // GLA_HEADS
    o_v = 2 * mw
    o_o = 3 * mw
    o_gq = 4 * mw
    o_gk = o_gq + gwk
    o_gv = o_gk + gwk
    o_gr = o_gv + gwv
    o_sm = o_gr + gwv

    def stage_a_pieces(x_ref, raw, slot):
        uqk, v1, uo, ugq, ugk, ugv, ugr, usm, ubg = raw

        def norm():
            h_s[...] = _rmsnorm(x_ref[0], g_ref[...]).astype(BF16)

        def proj(lo, hi):
            return _dot(h_s[...], w_ref[:, lo:hi])

        def p_q():
            uqk[pl.ds(8, ts), 0:mw] = proj(0, mw)

        def p_k():
            uqk[pl.ds(8, ts), mw:2 * mw] = proj(mw, o_v)

        def p_v():
            v = proj(o_v, o_o).astype(BF16)
            for hh in range(MLSTM_HEADS):
                v1[:, 2 * hh * hd:(2 * hh + 1) * hd] = v[:, hh * hd:(hh + 1) * hd]

        def p_o():
            uo[...] = proj(o_o, o_gq)

        def p_gqk():
            u = proj(o_gq, o_gv)
            ugq[...] = u[:, 0:gwk]
            ugk[...] = u[:, gwk:2 * gwk]

        def p_gv():
            ugv[...] = proj(o_gv, o_gr).astype(BF16)

        def p_gr():
            ugr[...] = proj(o_gr, o_sm)

        def p_sm():
            usm[...] = proj(o_sm, o_sm + LANES)

        def p_bg():
            small = usm[...] + sbias_ref[...]
            la = _log_sigmoid(_dot(small.astype(BF16), wa2_ref[...]) + ba_ref[...]) * (1.0 / GLA_TAU)
            bg = _segment_scan(la, CHUNK, jnp.add, 0.0)
            ubg[...] = bg
            flag_s[slot] = (jnp.min(bg) < -GLA_SAFE_LOG_RANGE).astype(jnp.int32)

        return [norm, p_q, p_k, p_v, p_o, p_gqk, p_gv, p_gr, p_sm, p_bg]

    def gla_intra_exact(raw):
        ugq, ugk, ugv, ubg = raw[3], raw[4], raw[5], raw[8]
        s_idx = lax.broadcasted_iota(jnp.int32, (CHUNK, gwk), 0)
        t_idx = lax.broadcasted_iota(jnp.int32, (CHUNK, gwv), 0)
        ind =(lax.broadcasted_iota(jnp.int32, (gwk, LANES), 0) // gdk
               == lax.broadcasted_iota(jnp.int32, (gwk, LANES), 1)).astype(BF16)

        def chunk_body(c, carry):
            r0 = pl.multiple_of(c * CHUNK, CHUNK)
            bgc = ubg[pl.ds(r0, CHUNK), :]
            gkc = ugk[pl.ds(r0, CHUNK), :]
            gvc = ugv[pl.ds(r0, CHUNK), :]

            def row_body(t, acc):
                bgt = ubg[pl.ds(r0 + t, 1), :]
                gqt = ugq[pl.ds(r0 + t, 1), :] * (gdk ** -0.5)
                valid = s_idx <= t
                e = jnp.exp(jnp.where(valid, bgt - bgc, 0.0))
                g = jnp.where(valid, gqt * gkc * e, 0.0).astype(BF16)
                a_cols = _dot(g, ind).astype(BF16)
                res = _dot_tn(a_cols, gvc)
                row = jnp.concatenate(
                    [res[hh:hh + 1, hh * gdv:(hh + 1) * gdv] for hh in range(GLA_HEADS)], axis=1)
                return jnp.where(t_idx == t, row, acc)

            gintra_s[pl.ds(r0, CHUNK), :] = lax.fori_loop(
                0, CHUNK, row_body, jnp.zeros((CHUNK, gwv), F32))
            return carry

        lax.fori_loop(0, nc, chunk_body, 0)

    def stage_b(x_ref, raw, exact_intra, a_pieces):
        a_pieces = list(a_pieces)
        for _ in range(min(3, len(a_pieces))):
            a_pieces.pop(0)()
        uqk, v1_s, uo, gq_s, gk_s, gv_s, ugr, usm, bg_s = raw
        gate_s[:, 0:mw] = _sigmoid(uo[...])
        gr = ugr[...]
        gate_s[:, mw:mw + gwv] = gr * _sigmoid(gr)
        small = usm[...] + sbias_ref[...]

        b = pltpu.roll(_segment_scan(_log_sigmoid(small), CHUNK, jnp.add, 0.0),
                       LANES - MLSTM_HEADS, axis=1)
        a = small - b
        cm = _segment_scan(a, CHUNK, jnp.maximum, -jnp.inf)
        b_s[...] = b
        cm_s[...] = cm
        b_last = b_s[pl.ds(CHUNK - 1, nc, stride=CHUNK), :]
        cm_last = cm_s[pl.ds(CHUNK - 1, nc, stride=CHUNK), :]
        m = m_s[0:1, :]
        m_rows = []
        lw_rows = []
        for c in range(nc):
            bl = b_last[c:c + 1, :]
            gmax = bl + cm_last[c:c + 1, :]
            m_new = jnp.maximum(bl + m, gmax)
            dec_s[c:c + 1, :] = jnp.exp(bl + m - m_new)
            m_rows.append(jnp.broadcast_to(m, (CHUNK, LANES)))
            lw_rows.append(jnp.broadcast_to(gmax - m_new - cm_last[c:c + 1, :], (CHUNK, LANES)))
            m = m_new
        m_s[0:1, :] = m
        mc = jnp.concatenate(m_rows, axis=0)
        mx = jnp.maximum(cm, mc)
        mx_s[...] = mx
        emt_s[...] = jnp.exp(-(b + mx))
        at_s[...] = a.T
        wint = jnp.exp(mc - mx)
        wkey = jnp.exp(a + jnp.concatenate(lw_rows, axis=0))

        uqk[0:8, :] = carry_s[...]
        y = convb_ref[...] + uqk[pl.ds(8, ts), :] * convw_ref[3:4, :]
        y = y + uqk[pl.ds(7, ts), :] * convw_ref[2:3, :]
        y = y + uqk[pl.ds(6, ts), :] * convw_ref[1:2, :]
        y = y + uqk[pl.ds(5, ts), :] * convw_ref[0:1, :]
        carry_s[...] = uqk[pl.ds(ts, 8), :]
        y = y * _sigmoid(y)
        for hh in range(MLSTM_HEADS):
            cols = slice(hh * hd, (hh + 1) * hd)
            qf = y[:, hh * hd:(hh + 1) * hd] * (hd ** -0.5)
            kf = y[:, mw + hh * hd:mw + (hh + 1) * hd]
            q_s[:, cols] = qf.astype(BF16)
            qw_s[:, cols] = (qf * wint[:, hh:hh + 1]).astype(BF16)
            k_s[:, cols] = kf.astype(BF16)
            kw_s[:, cols] = (kf * wkey[:, hh:hh + 1]).astype(BF16)

        ti = lax.broadcasted_iota(jnp.int32, (CHUNK, CHUNK), 0)
        si = lax.broadcasted_iota(jnp.int32, (CHUNK, CHUNK), 1)
        causal = ti >= si
        lane = lax.broadcasted_iota(jnp.int32, (CHUNK, gwk), 1)

        for c in range(nc):
            rows = pl.ds(c * CHUNK, CHUNK)
            mxc = mx_s[rows, :]
            em = emt_s[rows, :]
            dec = dec_s[c:c + 1, :]
            qk = []
            rhs = []
            for hh in range(MLSTM_HEADS):
                cols = slice(hh * hd, (hh + 1) * hd)
                v1h = v1_s[rows, 2 * hh * hd:(2 * hh + 2) * hd]
                c_prev = c_s[hh]
                qk.append(_dot_nt(q_s[rows, cols], k_s[rows, cols]))
                rhs.append(jnp.concatenate([c_prev.astype(BF16), v1h], axis=0))
                c_s[hh] = dec[:, hh:hh + 1] * c_prev + _dot_tn(kw_s[rows, cols], v1h)
            for hh in range(MLSTM_HEADS):
                cols = slice(hh * hd, (hh + 1) * hd)
                col = slice(hh, hh + 1)
                a_row = at_s[hh:hh + 1, c * CHUNK:(c + 1) * CHUNK]
                pm = jnp.exp(jnp.where(causal, a_row - mxc[:, col], -jnp.inf))
                lhs = jnp.concatenate([qw_s[rows, cols], (qk[hh] * pm).astype(BF16)], axis=1)
                r = _dot(lhs, rhs[hh])
                hraw_s[rows, cols] = r[:, 0:hd] / jnp.maximum(jnp.abs(r[:, hd:2 * hd]), em[:, col])

            bg = bg_s[rows, :]
            bg_last = bg[CHUNK - 1:CHUNK, :]
            gq = gq_s[rows, :]
            gk = gk_s[rows, :]
            gv = gv_s[rows, :]
            qe = gq * jnp.exp(bg) * (gdk ** -0.5)
            ke = (gk * jnp.exp(-bg)).astype(BF16)
            kw = gk * jnp.exp(bg_last - bg)
            st_prev = st_s[...]
            qe_heads = []
            kw_heads = []
            for hh in range(GLA_HEADS):
                hm = (lane >= hh * gdk) & (lane < (hh + 1) * gdk)
                qe_heads.append(jnp.where(hm, qe, 0.0).astype(BF16))
                kw_heads.append(jnp.where(hm, kw, 0.0).astype(BF16))
            qe_stack = jnp.concatenate(qe_heads, axis=0)
            kw_stack = jnp.concatenate(kw_heads, axis=0)
            v_stack = jnp.concatenate(
                [gv[:, hh * gdv:(hh + 1) * gdv] for hh in range(GLA_HEADS)], axis=0)
            a_all = _dot_nt(qe_stack, ke)
            inter_all = _dot_nt(qe_stack, st_prev.astype(BF16))
            st_s[...] = st_prev * jnp.exp(bg_last) + _dot_tn(v_stack, kw_stack)
            for hh in range(GLA_HEADS):
                hrows = slice(hh * CHUNK, (hh + 1) * CHUNK)
                vcols = slice(mw + hh * gdv, mw + (hh + 1) * gdv)
                a_h = jnp.where(causal, a_all[hrows, :], 0.0).astype(BF16)
                intra = jnp.where(exact_intra, gintra_s[rows, hh * gdv:(hh + 1) * gdv],
                                  _dot(a_h, gv[:, hh * gdv:(hh + 1) * gdv]))
                hraw_s[rows, vcols] = intra + inter_all[hrows, :]
            if a_pieces:
                a_pieces.pop(0)()

        while a_pieces:
            a_pieces.pop(0)()

        parts = []
        for hh in range(MLSTM_HEADS + GLA_HEADS):
            hv = hraw_s[:, hh * hd:(hh + 1) * hd]
            parts.append(hv * lax.rsqrt(jnp.mean(hv * hv, axis=-1, keepdims=True) + EPS))
        normed = jnp.concatenate(parts, axis=1)
        gains = jnp.concatenate([mng_ref[...], gng_ref[...]], axis=1)
        mix = (normed * gains * gate_s[...]).astype(BF16)
        o_ref[0] = x_ref[0] + _dot(mix, wout_ref[...])

    j = pl.program_id(0)

    @pl.when(j == 0)
    def _():
        for raw in raw_sets:
            for hh in range(MLSTM_HEADS):
                raw[1][:, (2 * hh + 1) * hd:(2 * hh + 2) * hd] = jnp.ones((ts, hd), BF16)
        gintra_s[...] = jnp.zeros(gintra_s.shape, F32)
        flag_s[1] = jnp.int32(0)
        for piece in stage_a_pieces(xa_ref, raw_sets[0], 0):
            piece()

    @pl.when((j > 0) & ((j - 1) % nt == 0))
    def _():
        carry_s[...] = jnp.zeros(carry_s.shape, F32)
        c_s[...] = jnp.zeros(c_s.shape, F32)
        m_s[...] = jnp.zeros(m_s.shape, F32)
        st_s[...] = jnp.zeros(st_s.shape, F32)

    for parity in (0, 1):
        prev = 1 - parity

        @pl.when((j > 0) & (j % 2 == parity) & (flag_s[prev] != 0))
        def _():
            gla_intra_exact(raw_sets[prev])

        @pl.when((j > 0) & (j % 2 == parity))
        def _():
            stage_b(xb_ref, raw_sets[prev], flag_s[prev] != 0,
                    stage_a_pieces(xa_ref, raw_sets[parity], parity))


def _mixer(x, g, w_cat, sbias, conv_w, conv_b, wa2_pad, ba, mng, gng, w_out, *, ts, mw, gwk, gwv):
    B, S, D = x.shape
    ncol = w_cat.shape[1]
    nc = ts // CHUNK
    nt = S // ts
    n_tiles = B * nt
    hd = mw // MLSTM_HEADS
    gdv = gwv // GLA_HEADS
    assert gdv == hd and hd == LANES and gwk % LANES == 0 and S % ts == 0 and ts % (2 * CHUNK) == 0
    const = lambda j: (0, 0)

    def tile_a(j):
        t = jnp.minimum(j, n_tiles - 1)
        return (t // nt, t % nt, 0)

    def tile_b(j):
        t = jnp.maximum(j - 1, 0)
        return (t // nt, t % nt, 0)

    single = pl.Buffered(1)
    col_f32 = pltpu.VMEM((ts, LANES), F32)
    raw_set = [
        pltpu.VMEM((ts + 8, 2 * mw), F32),
        pltpu.VMEM((ts, 2 * mw), BF16),
        pltpu.VMEM((ts, mw), F32),
        pltpu.VMEM((ts, gwk), F32),
        pltpu.VMEM((ts, gwk), F32),
        pltpu.VMEM((ts, gwv), BF16),
        pltpu.VMEM((ts, gwv), F32),
        col_f32,
        pltpu.VMEM((ts, gwk), F32),
    ]
    assert len(raw_set) == N_RAW
    kern = functools.partial(_mixer_kernel, ts=ts, nt=nt, mw=mw, gwk=gwk, gwv=gwv)
    return pl.pallas_call(
        kern,
        grid=(n_tiles + 1,),
        in_specs=[
            pl.BlockSpec((1, ts, D), tile_a),
            pl.BlockSpec((1, ts, D), tile_b),
            pl.BlockSpec((1, D), const),
            pl.BlockSpec((D, ncol), const, pipeline_mode=single),
            pl.BlockSpec((1, LANES), const),
            pl.BlockSpec((CONV_W, 2 * mw), const),
            pl.BlockSpec((1, 2 * mw), const),
            pl.BlockSpec((LANES, gwk), const),
            pl.BlockSpec((1, gwk), const),
            pl.BlockSpec((1, mw), const),
            pl.BlockSpec((1, gwv), const),
            pl.BlockSpec((mw + gwv, D), const, pipeline_mode=single),
        ],
        out_specs=pl.BlockSpec((1, ts, D), tile_b),
        out_shape=jax.ShapeDtypeStruct((B, S, D), F32),
        scratch_shapes=raw_set + raw_set + [
            pltpu.SMEM((2,), jnp.int32),
            pltpu.VMEM((ts, gwv), F32),
            pltpu.VMEM((ts, D), BF16),
            pltpu.VMEM((8, 2 * mw), F32),
            pltpu.VMEM((ts, mw), BF16),
            pltpu.VMEM((ts, mw), BF16),
            pltpu.VMEM((ts, mw), BF16),
            pltpu.VMEM((ts, mw), BF16),
            pltpu.VMEM((ts, mw + gwv), F32),
            pltpu.VMEM((ts, mw + gwv), F32),
            col_f32,
            col_f32,
            pltpu.VMEM((LANES, ts), F32),
            col_f32,
            col_f32,
            pltpu.VMEM((max(nc, 8), LANES), F32),
            pltpu.VMEM((MLSTM_HEADS, hd, 2 * hd), F32),
            pltpu.VMEM((8, LANES), F32),
            pltpu.VMEM((gdv, gwk), F32),
        ],
        compiler_params=pltpu.CompilerParams(
            dimension_semantics=("arbitrary",), vmem_limit_bytes=VMEM_LIMIT_BYTES),
        name="mixer",
    )(x, x, g, w_cat, sbias, conv_w, conv_b, wa2_pad, ba, mng, gng, w_out)


def _attn_mlp_kernel(x_ref, k_ref, v_ref, xg_ref, wq_ref, wo_ref, mg_ref, w1_ref, w2_ref, fg_ref,
                     o_ref, *, ff_chunk):
    x = x_ref[...]
    D = x.shape[1]
    hd = D // XATTN_HEADS
    hq = _rmsnorm(x, xg_ref[...]).astype(BF16)
    q = (_dot(hq, wq_ref[...]) * (hd ** -0.5)).astype(BF16)
    atts = []
    for hh in range(XATTN_HEADS):
        cols = slice(hh * hd, (hh + 1) * hd)
        s = _dot_nt(q[:, cols], k_ref[0, :, cols])
        p = jnp.exp(s - jnp.max(s, axis=-1, keepdims=True))
        att = _dot(p.astype(BF16), v_ref[0, :, cols])
        atts.append((att / jnp.sum(p, axis=-1, keepdims=True)).astype(BF16))
    x = x + _dot(jnp.concatenate(atts, axis=1), wo_ref[...])

    hm = _rmsnorm(x, mg_ref[...]).astype(BF16)
    acc = x
    for c in range(w1_ref.shape[1] // ff_chunk):
        cols = slice(c * ff_chunk, (c + 1) * ff_chunk)
        t = jnp.maximum(_dot(hm, w1_ref[:, cols]), 0.0)
        acc = acc + _dot((t * t).astype(BF16), w2_ref[cols, :])
    o_ref[...] = _rmsnorm(acc, fg_ref[...])


def _attn_mlp(x2d, k, v, xg, wq, wo, mg, w1, w2, fg, *, tm, seq, ff_chunk):
    T, D = x2d.shape
    M = k.shape[1]
    F = w1.shape[1]
    assert T % tm == 0 and seq % tm == 0 and F % ff_chunk == 0
    per_b = seq // tm
    const = lambda i: (0, 0)
    single = pl.Buffered(1)
    kern = functools.partial(_attn_mlp_kernel, ff_chunk=ff_chunk)
    return pl.pallas_call(
        kern,
        grid=(T // tm,),
        in_specs=[
            pl.BlockSpec((tm, D), lambda i: (i, 0)),
            pl.BlockSpec((1, M, D), lambda i: (i // per_b, 0, 0)),
            pl.BlockSpec((1, M, D), lambda i: (i // per_b, 0, 0)),
            pl.BlockSpec((1, D), const),
            pl.BlockSpec((D, D), const, pipeline_mode=single),
            pl.BlockSpec((D, D), const, pipeline_mode=single),
            pl.BlockSpec((1, D), const),
            pl.BlockSpec((D, F), const, pipeline_mode=single),
            pl.BlockSpec((F, D), const, pipeline_mode=single),
            pl.BlockSpec((1, D), const),
        ],
        out_specs=pl.BlockSpec((tm, D), lambda i: (i, 0)),
        out_shape=jax.ShapeDtypeStruct((T, D), F32),
        compiler_params=pltpu.CompilerParams(
            dimension_semantics=("arbitrary",), vmem_limit_bytes=VMEM_LIMIT_BYTES),
        name="attn_mlp",
    )(x2d, k, v, xg, wq, wo, mg, w1, w2, fg)


MIXER_SEQ_TILE = 512
ATTN_MLP_TILE = 512
FF_CHUNK = 1024


def kernel(x, mem, mix_norm_g, w_in, conv_w, conv_b, mlstm_i_b, mlstm_f_b, mlstm_norm_g, gla_wa2, gla_ba,
           gla_norm_g, w_out, xattn_norm_g, mem_norm_g, wq_x, wk_x, wv_x, wo_x, mlp_norm_g, w1, w2,
           final_norm_g):
    assert w_in.shape[0] == 1, "single-layer block"
    B, S, D = x.shape
    mw = mlstm_norm_g.shape[1]
    gwv = gla_norm_g.shape[1]
    rank, gwk = gla_wa2.shape[1:]
    nh = mlstm_i_b.shape[1]
    assert nh == MLSTM_HEADS and rank == GLA_RANK
    w_in = w_in[0]

    o_mi = 4 * mw
    o_gq = o_mi + 2 * nh
    o_ga = o_gq + 2 * gwk + 2 * gwv
    small_w = jnp.concatenate(
        [w_in[:, o_mi:o_gq], w_in[:, o_ga:o_ga + rank],
         jnp.zeros((D, LANES - 2 * nh - rank), w_in.dtype)], axis=1)
    w_cat = jnp.concatenate([w_in[:, 0:o_mi], w_in[:, o_gq:o_ga], small_w], axis=1).astype(BF16)
    sbias = jnp.concatenate([mlstm_i_b[0], mlstm_f_b[0], jnp.zeros((LANES - 2 * nh,), F32)])[None, :]
    wa2_pad = jnp.zeros((LANES, gwk), F32).at[2 * nh:2 * nh + rank].set(gla_wa2[0]).astype(BF16)

    k_mem, v_mem = _mem_kv(mem, mem_norm_g, wk_x[0].astype(BF16), wv_x[0].astype(BF16))
    x = _mixer(x, mix_norm_g, w_cat, sbias, conv_w[0], conv_b, wa2_pad, gla_ba,
               mlstm_norm_g, gla_norm_g, w_out[0].astype(BF16),
               ts=MIXER_SEQ_TILE, mw=mw, gwk=gwk, gwv=gwv)
    out = _attn_mlp(x.reshape(B * S, D), k_mem, v_mem, xattn_norm_g, wq_x[0].astype(BF16),
                    wo_x[0].astype(BF16), mlp_norm_g, w1[0].astype(BF16), w2[0].astype(BF16),
                    final_norm_g[None, :], tm=ATTN_MLP_TILE, seq=S, ff_chunk=FF_CHUNK)
    return out.reshape(B, S, D)
```

```python
import functools

import jax
import jax.numpy as jnp
from jax import lax
from jax.experimental import pallas as pl
from jax.experimental.pallas import tpu as pltpu

EPS = 1e-6
CHUNK = 64
CONV_W = 4
MLSTM_HEADS = 4
GLA_HEADS = 4
GLA_RANK = 16
GLA_TAU = 16.0
XATTN_HEADS = 4
LANES = 128
VMEM_LIMIT_BYTES = 56 * 1024 * 1024

BF16 = jnp.bfloat16
F32 = jnp.float32


def _rmsnorm(x, g):
    return x * lax.rsqrt(jnp.mean(x * x, axis=-1, keepdims=True) + EPS) * g


def _log_sigmoid(z):
    return jnp.minimum(z, 0.0) - jnp.log(1.0 + jnp.exp(-jnp.abs(z)))


def _sigmoid(z):
    return 0.5 * jnp.tanh(0.5 * z) + 0.5


def _dot(a, b):
    return jnp.dot(a, b, preferred_element_type=F32)


def _dot_nt(a, b):
    return lax.dot_general(a, b, (((1,), (1,)), ((), ())), preferred_element_type=F32)


def _dot_tn(a, b):
    return lax.dot_general(a, b, (((0,), (0,)), ((), ())), preferred_element_type=F32)


def _segment_scan(x, seg, op, identity):
    row = lax.broadcasted_iota(jnp.int32, x.shape, 0) & (seg - 1)
    k = 1
    while k < seg:
        shifted = pltpu.roll(x, k, axis=0)
        x = op(x, jnp.where(row >= k, shifted, identity))
        k *= 2
    return x


def _mem_kv_kernel(mem_ref, g_ref, wk_ref, wv_ref, k_ref, v_ref):
    mn = _rmsnorm(mem_ref[0], g_ref[...]).astype(BF16)
    k_ref[0] = _dot(mn, wk_ref[...]).astype(BF16)
    v_ref[0] = _dot(mn, wv_ref[...]).astype(BF16)


def _mem_kv(mem, g, wk, wv):
    B, M, D = mem.shape
    const = lambda b: (0, 0)
    return pl.pallas_call(
        _mem_kv_kernel,
        grid=(B,),
        in_specs=[
            pl.BlockSpec((1, M, D), lambda b: (b, 0, 0)),
            pl.BlockSpec((1, D), const),
            pl.BlockSpec((D, D), const),
            pl.BlockSpec((D, D), const),
        ],
        out_specs=[
            pl.BlockSpec((1, M, D), lambda b: (b, 0, 0)),
            pl.BlockSpec((1, M, D), lambda b: (b, 0, 0)),
        ],
        out_shape=[jax.ShapeDtypeStruct((B, M, D), BF16)] * 2,
        compiler_params=pltpu.CompilerParams(
            dimension_semantics=("arbitrary",), vmem_limit_bytes=VMEM_LIMIT_BYTES),
        name="mem_kv",
    )(mem, g, wk, wv)


N_RAW = 9
GLA_SAFE_LOG_RANGE = 60.0


def _mixer_kernel(xa_ref, xb_ref, g_ref, w_ref, sbias_ref, convw_ref, convb_ref, wa2_ref, ba_ref,
                  mng_ref, gng_ref, wout_ref,
                  o_ref,
                  *scratch, ts, nt, mw, gwk, gwv):
    raw_sets = (scratch[0:N_RAW], scratch[N_RAW:2 * N_RAW])
    (flag_s, gintra_s, h_s, carry_s, q_s, qw_s, k_s, kw_s, gate_s, hraw_s, b_s, cm_s, at_s, mx_s,
     emt_s, dec_s, c_s, m_s, st_s) = scratch[2 * N_RAW:]
    nc = ts // CHUNK
    hd = mw // MLSTM_HEADS
    gdk = gwk // GLA_HEADS
    gdv = gwv // GLA_HEADS
    o_v = 2 * mw
    o_o = 3 * mw
    o_gq = 4 * mw
    o_gk = o_gq + gwk
    o_gv = o_gk + gwk
    o_gr = o_gv + gwv
    o_sm = o_gr + gwv

    def stage_a_pieces(x_ref, raw, slot):
        uqk, v1, uo, ugq, ugk, ugv, ugr, usm, ubg = raw

        def norm():
            h_s[...] = _rmsnorm(x_ref[0], g_ref[...]).astype(BF16)

        def proj(lo, hi):
            return _dot(h_s[...], w_ref[:, lo:hi])

        def p_q():
            uqk[pl.ds(8, ts), 0:mw] = proj(0, mw)

        def p_k():
            uqk[pl.ds(8, ts), mw:2 * mw] = proj(mw, o_v)

        def p_v():
            v = proj(o_v, o_o).astype(BF16)
            for hh in range(MLSTM_HEADS):
                v1[:, 2 * hh * hd:(2 * hh + 1) * hd] = v[:, hh * hd:(hh + 1) * hd]

        def p_o():
            uo[...] = proj(o_o, o_gq)

        def p_gqk():
            u = proj(o_gq, o_gv)
            ugq[...] = u[:, 0:gwk]
            ugk[...] = u[:, gwk:2 * gwk]

        def p_gv():
            ugv[...] = proj(o_gv, o_gr).astype(BF16)

        def p_gr():
            ugr[...] = proj(o_gr, o_sm)

        def p_sm():
            usm[...] = proj(o_sm, o_sm + LANES)

        def p_bg():
            small = usm[...] + sbias_ref[...]
            la = _log_sigmoid(_dot(small.astype(BF16), wa2_ref[...]) + ba_ref[...]) * (1.0 / GLA_TAU)
            bg = _segment_scan(la, CHUNK, jnp.add, 0.0)
            ubg[...] = bg
            flag_s[slot] = (jnp.min(bg) < -GLA_SAFE_LOG_RANGE).astype(jnp.int32)

        return [norm, p_q, p_k, p_v, p_o, p_gqk, p_gv, p_gr, p_sm, p_bg]

    def gla_intra_exact(raw):
        ugq, ugk, ugv, ubg = raw[3], raw[4], raw[5], raw[8]
        s_idx = lax.broadcasted_iota(jnp.int32, (CHUNK, gwk), 0)
        t_idx = lax.broadcasted_iota(jnp.int32, (CHUNK, gwv), 0)
        ind =(lax.broadcasted_iota(jnp.int32, (gwk, LANES), 0) // gdk
               == lax.broadcasted_iota(jnp.int32, (gwk, LANES), 1)).astype(BF16)

        def chunk_body(c, carry):
            r0 = pl.multiple_of(c * CHUNK, CHUNK)
            bgc = ubg[pl.ds(r0, CHUNK), :]
            gkc = ugk[pl.ds(r0, CHUNK), :]
            gvc = ugv[pl.ds(r0, CHUNK), :]

            def row_body(t, acc):
                bgt = ubg[pl.ds(r0 + t, 1), :]
                gqt = ugq[pl.ds(r0 + t, 1), :] * (gdk ** -0.5)
                valid = s_idx <= t
                e = jnp.exp(jnp.where(valid, bgt - bgc, 0.0))
                g = jnp.where(valid, gqt * gkc * e, 0.0).astype(BF16)
                a_cols = _dot(g, ind).astype(BF16)
                res = _dot_tn(a_cols, gvc)
                row = jnp.concatenate(
                    [res[hh:hh + 1, hh * gdv:(hh + 1) * gdv] for hh in range(GLA_HEADS)], axis=1)
                return jnp.where(t_idx == t, row, acc)

            gintra_s[pl.ds(r0, CHUNK), :] = lax.fori_loop(
                0, CHUNK, row_body, jnp.zeros((CHUNK, gwv), F32))
            return carry

        lax.fori_loop(0, nc, chunk_body, 0)

    def stage_b(x_ref, raw, exact_intra, a_pieces):
        a_pieces = list(a_pieces)
        for _ in range(min(3, len(a_pieces))):
            a_pieces.pop(0)()
        uqk, v1_s, uo, gq_s, gk_s, gv_s, ugr, usm, bg_s = raw
        gate_s[:, 0:mw] = _sigmoid(uo[...])
        gr = ugr[...]
        gate_s[:, mw:mw + gwv] = gr * _sigmoid(gr)
        small = usm[...] + sbias_ref[...]

        b = pltpu.roll(_segment_scan(_log_sigmoid(small), CHUNK, jnp.add, 0.0),
                       LANES - MLSTM_HEADS, axis=1)
        a = small - b
        cm = _segment_scan(a, CHUNK, jnp.maximum, -jnp.inf)
        b_s[...] = b
        cm_s[...] = cm
        b_last = b_s[pl.ds(CHUNK - 1, nc, stride=CHUNK), :]
        cm_last = cm_s[pl.ds(CHUNK - 1, nc, stride=CHUNK), :]
        m = m_s[0:1, :]
        m_rows = []
        lw_rows = []
        for c in range(nc):
            bl = b_last[c:c + 1, :]
            gmax = bl + cm_last[c:c + 1, :]
            m_new = jnp.maximum(bl + m, gmax)
            dec_s[c:c + 1, :] = jnp.exp(bl + m - m_new)
            m_rows.append(jnp.broadcast_to(m, (CHUNK, LANES)))
            lw_rows.append(jnp.broadcast_to(gmax - m_new - cm_last[c:c + 1, :], (CHUNK, LANES)))
            m = m_new
        m_s[0:1, :] = m
        mc = jnp.concatenate(m_rows, axis=0)
        mx = jnp.maximum(cm, mc)
        mx_s[...] = mx
        emt_s[...] = jnp.exp(-(b + mx))
        at_s[...] = a.T
        wint = jnp.exp(mc - mx)
        wkey = jnp.exp(a + jnp.concatenate(lw_rows, axis=0))

        uqk[0:8, :] = carry_s[...]
        y = convb_ref[...] + uqk[pl.ds(8, ts), :] * convw_ref[3:4, :]
        y = y + uqk[pl.ds(7, ts), :] * convw_ref[2:3, :]
        y = y + uqk[pl.ds(6, ts), :] * convw_ref[1:2, :]
        y = y + uqk[pl.ds(5, ts), :] * convw_ref[0:1, :]
        carry_s[...] = uqk[pl.ds(ts, 8), :]
        y = y * _sigmoid(y)
        for hh in range(MLSTM_HEADS):
            cols = slice(hh * hd, (hh + 1) * hd)
            qf = y[:, hh * hd:(hh + 1) * hd] * (hd ** -0.5)
            kf = y[:, mw + hh * hd:mw + (hh + 1) * hd]
            q_s[:, cols] = qf.astype(BF16)
            qw_s[:, cols] = (qf * wint[:, hh:hh + 1]).astype(BF16)
            k_s[:, cols] = kf.astype(BF16)
            kw_s[:, cols] = (kf * wkey[:, hh:hh + 1]).astype(BF16)

        def causal_lanes(width):
            ti = lax.broadcasted_iota(jnp.int32, (CHUNK, width), 0)
            li = lax.broadcasted_iota(jnp.int32, (CHUNK, width), 1)
            return ti >= (li & (CHUNK - 1))

        causal = causal_lanes(CHUNK)
        causal2 = causal_lanes(2 * CHUNK)
        first_half = lax.broadcasted_iota(jnp.int32, (CHUNK, 2 * CHUNK), 1) < CHUNK
        lane = lax.broadcasted_iota(jnp.int32, (CHUNK, gwk), 1)
        zk = jnp.zeros((CHUNK, hd), BF16)
        zv = jnp.zeros((CHUNK, 2 * hd), BF16)

        for c in range(nc):
            rows = pl.ds(c * CHUNK, CHUNK)
            pair_lanes = slice((c // 2) * 2 * CHUNK, (c // 2 + 1) * 2 * CHUNK)
            mxc = mx_s[rows, :]
            em = emt_s[rows, :]
            dec = dec_s[c:c + 1, :]
            s_pairs = []
            rhs = []
            for hp in range(MLSTM_HEADS // 2):
                h0, h1 = 2 * hp, 2 * hp + 1
                k_blocks = jnp.concatenate(
                    [jnp.concatenate([k_s[rows, h0 * hd:(h0 + 1) * hd], zk], axis=1),
                     jnp.concatenate([zk, k_s[rows, h1 * hd:(h1 + 1) * hd]], axis=1)], axis=0)
                qk = _dot_nt(q_s[rows, h0 * hd:(h1 + 1) * hd], k_blocks)
                r0 = at_s[h0:h0 + 1, pair_lanes]
                r1 = at_s[h1:h1 + 1, pair_lanes]
                if c % 2 == 0:
                    a_pair = jnp.where(first_half[0:1, :], r0, pltpu.roll(r1, CHUNK, axis=1))
                else:
                    a_pair = jnp.where(first_half[0:1, :], pltpu.roll(r0, CHUNK, axis=1), r1)
                mx_pair = jnp.where(first_half, mxc[:, h0:h0 + 1], mxc[:, h1:h1 + 1])
                pm = jnp.exp(jnp.where(causal2, a_pair - mx_pair, -jnp.inf))
                s_pairs.append((qk * pm).astype(BF16))
            for hh in range(MLSTM_HEADS):
                cols = slice(hh * hd, (hh + 1) * hd)
                v1h = v1_s[rows, 2 * hh * hd:(2 * hh + 2) * hd]
                c_prev = c_s[hh]
                v_rows = [v1h, zv] if hh % 2 == 0 else [zv, v1h]
                rhs.append(jnp.concatenate([c_prev.astype(BF16)] + v_rows, axis=0))
                c_s[hh] = dec[:, hh:hh + 1] * c_prev + _dot_tn(kw_s[rows, cols], v1h)
            for hh in range(MLSTM_HEADS):
                cols = slice(hh * hd, (hh + 1) * hd)
                lhs = jnp.concatenate([qw_s[rows, cols], s_pairs[hh // 2]], axis=1)
                r = _dot(lhs, rhs[hh])
                hraw_s[rows, cols] = r[:, 0:hd] / jnp.maximum(jnp.abs(r[:, hd:2 * hd]),
                                                              em[:, hh:hh + 1])

            bg = bg_s[rows, :]
            bg_last = bg[CHUNK - 1:CHUNK, :]
            gq = gq_s[rows, :]
            gk = gk_s[rows, :]
            gv = gv_s[rows, :]
            qe = gq * jnp.exp(bg) * (gdk ** -0.5)
            ke = (gk * jnp.exp(-bg)).astype(BF16)
            kw = gk * jnp.exp(bg_last - bg)
            st_prev = st_s[...]
            qe_heads = []
            kw_heads = []
            for hh in range(GLA_HEADS):
                hm = (lane >= hh * gdk) & (lane < (hh + 1) * gdk)
                qe_heads.append(jnp.where(hm, qe, 0.0).astype(BF16))
                kw_heads.append(jnp.where(hm, kw, 0.0).astype(BF16))
            qe_stack = jnp.concatenate(qe_heads, axis=0)
            kw_stack = jnp.concatenate(kw_heads, axis=0)
            v_stack = jnp.concatenate(
                [gv[:, hh * gdv:(hh + 1) * gdv] for hh in range(GLA_HEADS)], axis=0)
            a_all = _dot_nt(qe_stack, ke)
            inter_all = _dot_nt(qe_stack, st_prev.astype(BF16))
            st_s[...] = st_prev * jnp.exp(bg_last) + _dot_tn(v_stack, kw_stack)
            for hh in range(GLA_HEADS):
                hrows = slice(hh * CHUNK, (hh + 1) * CHUNK)
                gcols = slice(hh * gdv, (hh + 1) * gdv)
                a_h = jnp.where(causal, a_all[hrows, :], 0.0).astype(BF16)
                intra = jnp.where(exact_intra, gintra_s[rows, gcols], _dot(a_h, gv[:, gcols]))
                hraw_s[rows, mw + hh * gdv:mw + (hh + 1) * gdv] = intra + inter_all[hrows, :]
            if a_pieces:
                a_pieces.pop(0)()

        while a_pieces:
            a_pieces.pop(0)()

        parts = []
        for hh in range(MLSTM_HEADS + GLA_HEADS):
            hv = hraw_s[:, hh * hd:(hh + 1) * hd]
            parts.append(hv * lax.rsqrt(jnp.mean(hv * hv, axis=-1, keepdims=True) + EPS))
        normed = jnp.concatenate(parts, axis=1)
        gains = jnp.concatenate([mng_ref[...], gng_ref[...]], axis=1)
        mix = (normed * gains * gate_s[...]).astype(BF16)
        o_ref[0] = x_ref[0] + _dot(mix, wout_ref[...])

    j = pl.program_id(0)

    @pl.when(j == 0)
    def _():
        for raw in raw_sets:
            for hh in range(MLSTM_HEADS):
                raw[1][:, (2 * hh + 1) * hd:(2 * hh + 2) * hd] = jnp.ones((ts, hd), BF16)
        gintra_s[...] = jnp.zeros(gintra_s.shape, F32)
        flag_s[1] = jnp.int32(0)
        for piece in stage_a_pieces(xa_ref, raw_sets[0], 0):
            piece()

    @pl.when((j > 0) & ((j - 1) % nt == 0))
    def _():
        carry_s[...] = jnp.zeros(carry_s.shape, F32)
        c_s[...] = jnp.zeros(c_s.shape, F32)
        m_s[...] = jnp.zeros(m_s.shape, F32)
        st_s[...] = jnp.zeros(st_s.shape, F32)

    for parity in (0, 1):
        prev = 1 - parity

        @pl.when((j > 0) & (j % 2 == parity) & (flag_s[prev] != 0))
        def _():
            gla_intra_exact(raw_sets[prev])

        @pl.when((j > 0) & (j % 2 == parity))
        def _():
            stage_b(xb_ref, raw_sets[prev], flag_s[prev] != 0,
                    stage_a_pieces(xa_ref, raw_sets[parity], parity))


def _mixer(x, g, w_cat, sbias, conv_w, conv_b, wa2_pad, ba, mng, gng, w_out, *, ts, mw, gwk, gwv):
    B, S, D = x.shape
    ncol = w_cat.shape[1]
    nc = ts // CHUNK
    nt = S // ts
    n_tiles = B * nt
    hd = mw // MLSTM_HEADS
    gdv = gwv // GLA_HEADS
    assert gdv == hd and hd == LANES and gwk % LANES == 0 and S % ts == 0 and ts % (2 * CHUNK) == 0
    const = lambda j: (0, 0)

    def tile_a(j):
        t = jnp.minimum(j, n_tiles - 1)
        return (t // nt, t % nt, 0)

    def tile_b(j):
        t = jnp.maximum(j - 1, 0)
        return (t // nt, t % nt, 0)

    single = pl.Buffered(1)
    col_f32 = pltpu.VMEM((ts, LANES), F32)
    raw_set = [
        pltpu.VMEM((ts + 8, 2 * mw), F32),
        pltpu.VMEM((ts, 2 * mw), BF16),
        pltpu.VMEM((ts, mw), F32),
        pltpu.VMEM((ts, gwk), F32),
        pltpu.VMEM((ts, gwk), F32),
        pltpu.VMEM((ts, gwv), BF16),
        pltpu.VMEM((ts, gwv), F32),
        col_f32,
        pltpu.VMEM((ts, gwk), F32),
    ]
    assert len(raw_set) == N_RAW
    kern = functools.partial(_mixer_kernel, ts=ts, nt=nt, mw=mw, gwk=gwk, gwv=gwv)
    return pl.pallas_call(
        kern,
        grid=(n_tiles + 1,),
        in_specs=[
            pl.BlockSpec((1, ts, D), tile_a),
            pl.BlockSpec((1, ts, D), tile_b),
            pl.BlockSpec((1, D), const),
            pl.BlockSpec((D, ncol), const, pipeline_mode=single),
            pl.BlockSpec((1, LANES), const),
            pl.BlockSpec((CONV_W, 2 * mw), const),
            pl.BlockSpec((1, 2 * mw), const),
            pl.BlockSpec((LANES, gwk), const),
            pl.BlockSpec((1, gwk), const),
            pl.BlockSpec((1, mw), const),
            pl.BlockSpec((1, gwv), const),
            pl.BlockSpec((mw + gwv, D), const, pipeline_mode=single),
        ],
        out_specs=pl.BlockSpec((1, ts, D), tile_b),
        out_shape=jax.ShapeDtypeStruct((B, S, D), F32),
        scratch_shapes=raw_set + raw_set + [
            pltpu.SMEM((2,), jnp.int32),
            pltpu.VMEM((ts, gwv), F32),
            pltpu.VMEM((ts, D), BF16),
            pltpu.VMEM((8, 2 * mw), F32),
            pltpu.VMEM((ts, mw), BF16),
            pltpu.VMEM((ts, mw), BF16),
            pltpu.VMEM((ts, mw), BF16),
            pltpu.VMEM((ts, mw), BF16),
            pltpu.VMEM((ts, mw + gwv), F32),
            pltpu.VMEM((ts, mw + gwv), F32),
            col_f32,
            col_f32,
            pltpu.VMEM((LANES, ts), F32),
            col_f32,
            col_f32,
            pltpu.VMEM((max(nc, 8), LANES), F32),
            pltpu.VMEM((MLSTM_HEADS, hd, 2 * hd), F32),
            pltpu.VMEM((8, LANES), F32),
            pltpu.VMEM((gdv, gwk), F32),
        ],
        compiler_params=pltpu.CompilerParams(
            dimension_semantics=("arbitrary",), vmem_limit_bytes=VMEM_LIMIT_BYTES),
        name="mixer",
    )(x, x, g, w_cat, sbias, conv_w, conv_b, wa2_pad, ba, mng, gng, w_out)


def _attn_mlp_kernel(x_ref, k_ref, v_ref, xg_ref, wq_ref, wo_ref, mg_ref, w1_ref, w2_ref, fg_ref,
                     o_ref, *, ff_chunk):
    x = x_ref[...]
    D = x.shape[1]
    hd = D // XATTN_HEADS
    hq = _rmsnorm(x, xg_ref[...]).astype(BF16)
    q = (_dot(hq, wq_ref[...]) * (hd ** -0.5)).astype(BF16)
    atts = []
    for hh in range(XATTN_HEADS):
        cols = slice(hh * hd, (hh + 1) * hd)
        s = _dot_nt(q[:, cols], k_ref[0, :, cols])
        p = jnp.exp(s - jnp.max(s, axis=-1, keepdims=True))
        att = _dot(p.astype(BF16), v_ref[0, :, cols])
        atts.append((att / jnp.sum(p, axis=-1, keepdims=True)).astype(BF16))
    x = x + _dot(jnp.concatenate(atts, axis=1), wo_ref[...])

    hm = _rmsnorm(x, mg_ref[...]).astype(BF16)
    acc = x
    for c in range(w1_ref.shape[1] // ff_chunk):
        cols = slice(c * ff_chunk, (c + 1) * ff_chunk)
        t = jnp.maximum(_dot(hm, w1_ref[:, cols]), 0.0)
        acc = acc + _dot((t * t).astype(BF16), w2_ref[cols, :])
    o_ref[...] = _rmsnorm(acc, fg_ref[...])


def _attn_mlp(x2d, k, v, xg, wq, wo, mg, w1, w2, fg, *, tm, seq, ff_chunk):
    T, D = x2d.shape
    M = k.shape[1]
    F = w1.shape[1]
    assert T % tm == 0 and seq % tm == 0 and F % ff_chunk == 0
    per_b = seq // tm
    const = lambda i: (0, 0)
    single = pl.Buffered(1)
    kern = functools.partial(_attn_mlp_kernel, ff_chunk=ff_chunk)
    return pl.pallas_call(
        kern,
        grid=(T // tm,),
        in_specs=[
            pl.BlockSpec((tm, D), lambda i: (i, 0)),
            pl.BlockSpec((1, M, D), lambda i: (i // per_b, 0, 0)),
            pl.BlockSpec((1, M, D), lambda i: (i // per_b, 0, 0)),
            pl.BlockSpec((1, D), const),
            pl.BlockSpec((D, D), const, pipeline_mode=single),
            pl.BlockSpec((D, D), const, pipeline_mode=single),
            pl.BlockSpec((1, D), const),
            pl.BlockSpec((D, F), const, pipeline_mode=single),
            pl.BlockSpec((F, D), const, pipeline_mode=single),
            pl.BlockSpec((1, D), const),
        ],
        out_specs=pl.BlockSpec((tm, D), lambda i: (i, 0)),
        out_shape=jax.ShapeDtypeStruct((T, D), F32),
        compiler_params=pltpu.CompilerParams(
            dimension_semantics=("arbitrary",), vmem_limit_bytes=VMEM_LIMIT_BYTES),
        name="attn_mlp",
    )(x2d, k, v, xg, wq, wo, mg, w1, w2, fg)


MIXER_SEQ_TILE = 512
ATTN_MLP_TILE = 1024
FF_CHUNK = 1024


def kernel(x, mem, mix_norm_g, w_in, conv_w, conv_b, mlstm_i_b, mlstm_f_b, mlstm_norm_g, gla_wa2, gla_ba,
           gla_norm_g, w_out, xattn_norm_g, mem_norm_g, wq_x, wk_x, wv_x, wo_x, mlp_norm_g, w1, w2,
           final_norm_g):
    assert w_in.shape[0] == 1, "single-layer block"
    B, S, D = x.shape
    mw = mlstm_norm_g.shape[1]
    gwv = gla_norm_g.shape[1]
    rank, gwk = gla_wa2.shape[1:]
    nh = mlstm_i_b.shape[1]
    assert nh == MLSTM_HEADS and rank == GLA_RANK
    w_in = w_in[0]

    o_mi = 4 * mw
    o_gq = o_mi + 2 * nh
    o_ga = o_gq + 2 * gwk + 2 * gwv
    small_w = jnp.concatenate(
        [w_in[:, o_mi:o_gq], w_in[:, o_ga:o_ga + rank],
         jnp.zeros((D, LANES - 2 * nh - rank), w_in.dtype)], axis=1)
    w_cat = jnp.concatenate([w_in[:, 0:o_mi], w_in[:, o_gq:o_ga], small_w], axis=1).astype(BF16)
    sbias = jnp.concatenate([mlstm_i_b[0], mlstm_f_b[0], jnp.zeros((LANES - 2 * nh,), F32)])[None, :]
    wa2_pad = jnp.zeros((LANES, gwk), F32).at[2 * nh:2 * nh + rank].set(gla_wa2[0]).astype(BF16)

    k_mem, v_mem = _mem_kv(mem, mem_norm_g, wk_x[0].astype(BF16), wv_x[0].astype(BF16))
    x = _mixer(x, mix_norm_g, w_cat, sbias, conv_w[0], conv_b, wa2_pad, gla_ba,
               mlstm_norm_g, gla_norm_g, w_out[0].astype(BF16),
               ts=MIXER_SEQ_TILE, mw=mw, gwk=gwk, gwv=gwv)
    out = _attn_mlp(x.reshape(B * S, D), k_mem, v_mem, xattn_norm_g, wq_x[0].astype(BF16),
                    wo_x[0].astype(BF16), mlp_norm_g, w1[0].astype(BF16), w2[0].astype(BF16),
                    final_norm_g[None, :], tm=ATTN_MLP_TILE, seq=S, ff_chunk=FF_CHUNK)
    return out.reshape(B, S, D)
```

```python
import functools

import jax
import jax.numpy as jnp
from jax import lax
from jax.experimental import pallas as pl
from jax.experimental.pallas import tpu as pltpu

EPS = 1e-6
CHUNK = 64
CONV_W = 4
MLSTM_HEADS = 4
GLA_HEADS = 4
GLA_RANK = 16
GLA_TAU = 16.0
XATTN_HEADS = 4
LANES = 128
MXU_COLS = 256
VMEM_LIMIT_BYTES = 56 * 1024 * 1024

BF16 = jnp.bfloat16
F32 = jnp.float32


def _rmsnorm(x, g):
    return x * lax.rsqrt(jnp.mean(x * x, axis=-1, keepdims=True) + EPS) * g


def _log_sigmoid(z):
    return jnp.minimum(z, 0.0) - jnp.log(1.0 + jnp.exp(-jnp.abs(z)))


def _sigmoid(z):
    return 0.5 * jnp.tanh(0.5 * z) + 0.5


def _dot(a, b):
    return jnp.dot(a, b, preferred_element_type=F32)


def _dot_nt(a, b):
    return lax.dot_general(a, b, (((1,), (1,)), ((), ())), preferred_element_type=F32)


def _dot_tn(a, b):
    return lax.dot_general(a, b, (((0,), (0,)), ((), ())), preferred_element_type=F32)


def _segment_scan(x, seg, op, identity):
    row = lax.broadcasted_iota(jnp.int32, x.shape, 0) & (seg - 1)
    k = 1
    while k < seg:
        shifted = pltpu.roll(x, k, axis=0)
        x = op(x, jnp.where(row >= k, shifted, identity))
        k *= 2
    return x


def _mem_kv_kernel(mem_ref, g_ref, wk_ref, wv_ref, k_ref, v_ref):
    mn = _rmsnorm(mem_ref[0], g_ref[...]).astype(BF16)
    k_ref[0] = _dot(mn, wk_ref[...]).astype(BF16)
    v_ref[0] = _dot(mn, wv_ref[...]).astype(BF16)


def _mem_kv(mem, g, wk, wv):
    B, M, D = mem.shape
    const = lambda b: (0, 0)
    return pl.pallas_call(
        _mem_kv_kernel,
        grid=(B,),
        in_specs=[
            pl.BlockSpec((1, M, D), lambda b: (b, 0, 0)),
            pl.BlockSpec((1, D), const),
            pl.BlockSpec((D, D), const),
            pl.BlockSpec((D, D), const),
        ],
        out_specs=[
            pl.BlockSpec((1, M, D), lambda b: (b, 0, 0)),
            pl.BlockSpec((1, M, D), lambda b: (b, 0, 0)),
        ],
        out_shape=[jax.ShapeDtypeStruct((B, M, D), BF16)] * 2,
        compiler_params=pltpu.CompilerParams(
            dimension_semantics=("arbitrary",), vmem_limit_bytes=VMEM_LIMIT_BYTES),
        name="mem_kv",
    )(mem, g, wk, wv)


N_RAW = 9
GLA_SAFE_LOG_RANGE = 60.0


def _mixer_kernel(xa_ref, xb_ref, g_ref, w_ref, sbias_ref, convw_ref, convb_ref, wa2_ref, ba_ref,
                  mng_ref, gng_ref, wout_ref,
                  o_ref,
                  *scratch, ts, nt, mw, gwk, gwv):
    raw_sets = (scratch[0:N_RAW], scratch[N_RAW:2 * N_RAW])
    (flag_s, gintra_s, h_s, carry_s, q_s, qw_s, k_s, kw_s, gate_s, hraw_s, b_s, cm_s, at_s, mx_s,
     emt_s, dec_s, c_s, m_s, st_s) = scratch[2 * N_RAW:]
    nc = ts // CHUNK
    hd = mw // MLSTM_HEADS
    gdk = gwk // GLA_HEADS
    gdv = gwv // GLA_HEADS
    o_v = 2 * mw
    o_o = 3 * mw
    o_gq = 4 * mw
    o_gk = o_gq + gwk
    o_gv = o_gk + gwk
    o_gr = o_gv + gwv
    o_sm = o_gr + gwv

    def stage_a_pieces(x_ref, raw, slot):
        uqk, v1, uo, ugq, ugk, ugv, ugr, usm, ubg = raw

        def norm():
            h_s[...] = _rmsnorm(x_ref[0], g_ref[...]).astype(BF16)

        def proj(lo, hi):
            return _dot(h_s[...], w_ref[:, lo:hi])

        def piece(lo, width, store):
            return lambda: store(proj(lo, lo + width))

        def store_to(ref, row0, c0, dtype):
            def store(u):
                ref[pl.ds(row0, ts), c0:c0 + u.shape[1]] = u.astype(dtype)
            return store

        def store_v(first_head):
            def store(u):
                for i in range(u.shape[1] // hd):
                    hh = first_head + i
                    v1[:, 2 * hh * hd:(2 * hh + 1) * hd] = u[:, i * hd:(i + 1) * hd].astype(BF16)
            return store

        w = MXU_COLS
        pieces = []
        for c0 in range(0, 2 * mw, w):
            pieces.append(piece(c0, w, store_to(uqk, 8, c0, F32)))
        for c0 in range(0, mw, w):
            pieces.append(piece(o_v + c0, w, store_v(c0 // hd)))
        for c0 in range(0, mw, w):
            pieces.append(piece(o_o + c0, w, store_to(uo, 0, c0, F32)))
        for c0 in range(0, gwk, w):
            pieces.append(piece(o_gq + c0, w, store_to(ugq, 0, c0, F32)))
        for c0 in range(0, gwk, w):
            pieces.append(piece(o_gk + c0, w, store_to(ugk, 0, c0, F32)))
        for c0 in range(0, gwv, w):
            pieces.append(piece(o_gv + c0, w, store_to(ugv, 0, c0, BF16)))
        for c0 in range(0, gwv, w):
            pieces.append(piece(o_gr + c0, w, store_to(ugr, 0, c0, F32)))
        pieces.append(piece(o_sm, LANES, store_to(usm, 0, 0, F32)))

        def p_bg():
            small = usm[...] + sbias_ref[...]
            la = _log_sigmoid(_dot(small.astype(BF16), wa2_ref[...]) + ba_ref[...]) * (1.0 / GLA_TAU)
            bg = _segment_scan(la, CHUNK, jnp.add, 0.0)
            ubg[...] = bg
            flag_s[slot] = (jnp.min(bg) < -GLA_SAFE_LOG_RANGE).astype(jnp.int32)

        return [norm] + pieces + [p_bg]

    def gla_intra_exact(raw):
        ugq, ugk, ugv, ubg = raw[3], raw[4], raw[5], raw[8]
        s_idx = lax.broadcasted_iota(jnp.int32, (CHUNK, gwk), 0)
        t_idx = lax.broadcasted_iota(jnp.int32, (CHUNK, gwv), 0)
        ind =(lax.broadcasted_iota(jnp.int32, (gwk, LANES), 0) // gdk
               == lax.broadcasted_iota(jnp.int32, (gwk, LANES), 1)).astype(BF16)

        def chunk_body(c, carry):
            r0 = pl.multiple_of(c * CHUNK, CHUNK)
            bgc = ubg[pl.ds(r0, CHUNK), :]
            gkc = ugk[pl.ds(r0, CHUNK), :]
            gvc = ugv[pl.ds(r0, CHUNK), :]

            def row_body(t, acc):
                bgt = ubg[pl.ds(r0 + t, 1), :]
                gqt = ugq[pl.ds(r0 + t, 1), :] * (gdk ** -0.5)
                valid = s_idx <= t
                e = jnp.exp(jnp.where(valid, bgt - bgc, 0.0))
                g = jnp.where(valid, gqt * gkc * e, 0.0).astype(BF16)
                a_cols = _dot(g, ind).astype(BF16)
                res = _dot_tn(a_cols, gvc)
                row = jnp.concatenate(
                    [res[hh:hh + 1, hh * gdv:(hh + 1) * gdv] for hh in range(GLA_HEADS)], axis=1)
                return jnp.where(t_idx == t, row, acc)

            gintra_s[pl.ds(r0, CHUNK), :] = lax.fori_loop(
                0, CHUNK, row_body, jnp.zeros((CHUNK, gwv), F32))
            return carry

        lax.fori_loop(0, nc, chunk_body, 0)

    def stage_b(x_ref, raw, exact_intra, a_pieces):
        a_pieces = list(a_pieces)
        for _ in range(1 + 2 * mw // MXU_COLS):
            a_pieces.pop(0)()
        uqk, v1_s, uo, gq_s, gk_s, gv_s, ugr, usm, bg_s = raw
        gate_s[:, 0:mw] = _sigmoid(uo[...])
        gr = ugr[...]
        gate_s[:, mw:mw + gwv] = gr * _sigmoid(gr)
        small = usm[...] + sbias_ref[...]

        b = pltpu.roll(_segment_scan(_log_sigmoid(small), CHUNK, jnp.add, 0.0),
                       LANES - MLSTM_HEADS, axis=1)
        a = small - b
        cm = _segment_scan(a, CHUNK, jnp.maximum, -jnp.inf)
        b_s[...] = b
        cm_s[...] = cm
        b_last = b_s[pl.ds(CHUNK - 1, nc, stride=CHUNK), :]
        cm_last = cm_s[pl.ds(CHUNK - 1, nc, stride=CHUNK), :]
        m = m_s[0:1, :]
        m_rows = []
        lw_rows = []
        for c in range(nc):
            bl = b_last[c:c + 1, :]
            gmax = bl + cm_last[c:c + 1, :]
            m_new = jnp.maximum(bl + m, gmax)
            dec_s[c:c + 1, :] = jnp.exp(bl + m - m_new)
            m_rows.append(jnp.broadcast_to(m, (CHUNK, LANES)))
            lw_rows.append(jnp.broadcast_to(gmax - m_new - cm_last[c:c + 1, :], (CHUNK, LANES)))
            m = m_new
        m_s[0:1, :] = m
        mc = jnp.concatenate(m_rows, axis=0)
        mx = jnp.maximum(cm, mc)
        mx_s[...] = mx
        emt_s[...] = jnp.exp(-(b + mx))
        at_s[...] = a.T
        wint = jnp.exp(mc - mx)
        wkey = jnp.exp(a + jnp.concatenate(lw_rows, axis=0))

        uqk[0:8, :] = carry_s[...]
        y = convb_ref[...] + uqk[pl.ds(8, ts), :] * convw_ref[3:4, :]
        y = y + uqk[pl.ds(7, ts), :] * convw_ref[2:3, :]
        y = y + uqk[pl.ds(6, ts), :] * convw_ref[1:2, :]
        y = y + uqk[pl.ds(5, ts), :] * convw_ref[0:1, :]
        carry_s[...] = uqk[pl.ds(ts, 8), :]
        y = y * _sigmoid(y)
        for hh in range(MLSTM_HEADS):
            cols = slice(hh * hd, (hh + 1) * hd)
            qf = y[:, hh * hd:(hh + 1) * hd] * (hd ** -0.5)
            kf = y[:, mw + hh * hd:mw + (hh + 1) * hd]
            q_s[:, cols] = qf.astype(BF16)
            qw_s[:, cols] = (qf * wint[:, hh:hh + 1]).astype(BF16)
            k_s[:, cols] = kf.astype(BF16)
            kw_s[:, cols] = (kf * wkey[:, hh:hh + 1]).astype(BF16)

        ti = lax.broadcasted_iota(jnp.int32, (CHUNK, CHUNK), 0)
        si = lax.broadcasted_iota(jnp.int32, (CHUNK, CHUNK), 1)
        causal = ti >= si
        lane = lax.broadcasted_iota(jnp.int32, (CHUNK, gwk), 1)

        def level1(c):
            rows = pl.ds(c * CHUNK, CHUNK)
            dec = dec_s[c:c + 1, :]
            qk = []
            rhs = []
            for hh in range(MLSTM_HEADS):
                cols = slice(hh * hd, (hh + 1) * hd)
                v1h = v1_s[rows, 2 * hh * hd:(2 * hh + 2) * hd]
                c_prev = c_s[hh]
                qk.append(_dot_nt(q_s[rows, cols], k_s[rows, cols]))
                rhs.append(jnp.concatenate([c_prev.astype(BF16), v1h], axis=0))
                c_s[hh] = dec[:, hh:hh + 1] * c_prev + _dot_tn(kw_s[rows, cols], v1h)
            bg = bg_s[rows, :]
            bg_last = bg[CHUNK - 1:CHUNK, :]
            gq = gq_s[rows, :]
            gk = gk_s[rows, :]
            gv = gv_s[rows, :]
            qe = gq * jnp.exp(bg) * (gdk ** -0.5)
            ke = (gk * jnp.exp(-bg)).astype(BF16)
            kw = gk * jnp.exp(bg_last - bg)
            st_prev = st_s[...]
            qe_heads = []
            kw_heads = []
            for hh in range(GLA_HEADS):
                hm = (lane >= hh * gdk) & (lane < (hh + 1) * gdk)
                qe_heads.append(jnp.where(hm, qe, 0.0).astype(BF16))
                kw_heads.append(jnp.where(hm, kw, 0.0).astype(BF16))
            qe_stack = jnp.concatenate(qe_heads, axis=0)
            kw_stack = jnp.concatenate(kw_heads, axis=0)
            v_stack = jnp.concatenate(
                [gv[:, hh * gdv:(hh + 1) * gdv] for hh in range(GLA_HEADS)], axis=0)
            a_all = _dot_nt(qe_stack, ke)
            inter_all = _dot_nt(qe_stack, st_prev.astype(BF16))
            st_s[...] = st_prev * jnp.exp(bg_last) + _dot_tn(v_stack, kw_stack)
            return qk, rhs, a_all, inter_all

        def level2(c, qk, rhs, a_all, inter_all):
            rows = pl.ds(c * CHUNK, CHUNK)
            mxc = mx_s[rows, :]
            em = emt_s[rows, :]
            gv = gv_s[rows, :]
            for hh in range(MLSTM_HEADS):
                cols = slice(hh * hd, (hh + 1) * hd)
                col = slice(hh, hh + 1)
                a_row = at_s[hh:hh + 1, c * CHUNK:(c + 1) * CHUNK]
                pm = jnp.exp(jnp.where(causal, a_row - mxc[:, col], -jnp.inf))
                lhs = jnp.concatenate([qw_s[rows, cols], (qk[hh] * pm).astype(BF16)], axis=1)
                r = _dot(lhs, rhs[hh])
                hraw_s[rows, cols] = r[:, 0:hd] / jnp.maximum(jnp.abs(r[:, hd:2 * hd]), em[:, col])
            for hh in range(GLA_HEADS):
                hrows = slice(hh * CHUNK, (hh + 1) * CHUNK)
                gcols = slice(hh * gdv, (hh + 1) * gdv)
                a_h = jnp.where(causal, a_all[hrows, :], 0.0).astype(BF16)
                intra = jnp.where(exact_intra, gintra_s[rows, gcols], _dot(a_h, gv[:, gcols]))
                hraw_s[rows, mw + hh * gdv:mw + (hh + 1) * gdv] = intra + inter_all[hrows, :]

        def next_a_piece():
            if a_pieces:
                a_pieces.pop(0)()

        pending = level1(0)
        for c in range(nc):
            current = pending
            if c + 1 < nc:
                pending = level1(c + 1)
                next_a_piece()
            level2(c, *current)
            next_a_piece()

        while a_pieces:
            a_pieces.pop(0)()

        parts = []
        for hh in range(MLSTM_HEADS + GLA_HEADS):
            hv = hraw_s[:, hh * hd:(hh + 1) * hd]
            parts.append(hv * lax.rsqrt(jnp.mean(hv * hv, axis=-1, keepdims=True) + EPS))
        normed = jnp.concatenate(parts, axis=1)
        gains = jnp.concatenate([mng_ref[...], gng_ref[...]], axis=1)
        mix = (normed * gains * gate_s[...]).astype(BF16)
        o_ref[0] = x_ref[0] + _dot(mix, wout_ref[...])

    j = pl.program_id(0)

    @pl.when(j == 0)
    def _():
        for raw in raw_sets:
            for hh in range(MLSTM_HEADS):
                raw[1][:, (2 * hh + 1) * hd:(2 * hh + 2) * hd] = jnp.ones((ts, hd), BF16)
        gintra_s[...] = jnp.zeros(gintra_s.shape, F32)
        flag_s[1] = jnp.int32(0)
        for piece in stage_a_pieces(xa_ref, raw_sets[0], 0):
            piece()

    @pl.when((j > 0) & ((j - 1) % nt == 0))
    def _():
        carry_s[...] = jnp.zeros(carry_s.shape, F32)
        c_s[...] = jnp.zeros(c_s.shape, F32)
        m_s[...] = jnp.zeros(m_s.shape, F32)
        st_s[...] = jnp.zeros(st_s.shape, F32)

    for parity in (0, 1):
        prev = 1 - parity

        @pl.when((j > 0) & (j % 2 == parity) & (flag_s[prev] != 0))
        def _():
            gla_intra_exact(raw_sets[prev])

        @pl.when((j > 0) & (j % 2 == parity))
        def _():
            stage_b(xb_ref, raw_sets[prev], flag_s[prev] != 0,
                    stage_a_pieces(xa_ref, raw_sets[parity], parity))


def _mixer(x, g, w_cat, sbias, conv_w, conv_b, wa2_pad, ba, mng, gng, w_out, *, ts, mw, gwk, gwv):
    B, S, D = x.shape
    ncol = w_cat.shape[1]
    nc = ts // CHUNK
    nt = S // ts
    n_tiles = B * nt
    hd = mw // MLSTM_HEADS
    gdv = gwv // GLA_HEADS
    assert gdv == hd and hd == LANES and gwk % LANES == 0 and S % ts == 0 and ts % (2 * CHUNK) == 0
    const = lambda j: (0, 0)

    def tile_a(j):
        t = jnp.minimum(j, n_tiles - 1)
        return (t // nt, t % nt, 0)

    def tile_b(j):
        t = jnp.maximum(j - 1, 0)
        return (t // nt, t % nt, 0)

    single = pl.Buffered(1)
    col_f32 = pltpu.VMEM((ts, LANES), F32)
    raw_set = [
        pltpu.VMEM((ts + 8, 2 * mw), F32),
        pltpu.VMEM((ts, 2 * mw), BF16),
        pltpu.VMEM((ts, mw), F32),
        pltpu.VMEM((ts, gwk), F32),
        pltpu.VMEM((ts, gwk), F32),
        pltpu.VMEM((ts, gwv), BF16),
        pltpu.VMEM((ts, gwv), F32),
        col_f32,
        pltpu.VMEM((ts, gwk), F32),
    ]
    assert len(raw_set) == N_RAW
    kern = functools.partial(_mixer_kernel, ts=ts, nt=nt, mw=mw, gwk=gwk, gwv=gwv)
    return pl.pallas_call(
        kern,
        grid=(n_tiles + 1,),
        in_specs=[
            pl.BlockSpec((1, ts, D), tile_a),
            pl.BlockSpec((1, ts, D), tile_b),
            pl.BlockSpec((1, D), const),
            pl.BlockSpec((D, ncol), const, pipeline_mode=single),
            pl.BlockSpec((1, LANES), const),
            pl.BlockSpec((CONV_W, 2 * mw), const),
            pl.BlockSpec((1, 2 * mw), const),
            pl.BlockSpec((LANES, gwk), const),
            pl.BlockSpec((1, gwk), const),
            pl.BlockSpec((1, mw), const),
            pl.BlockSpec((1, gwv), const),
            pl.BlockSpec((mw + gwv, D), const, pipeline_mode=single),
        ],
        out_specs=pl.BlockSpec((1, ts, D), tile_b),
        out_shape=jax.ShapeDtypeStruct((B, S, D), F32),
        scratch_shapes=raw_set + raw_set + [
            pltpu.SMEM((2,), jnp.int32),
            pltpu.VMEM((ts, gwv), F32),
            pltpu.VMEM((ts, D), BF16),
            pltpu.VMEM((8, 2 * mw), F32),
            pltpu.VMEM((ts, mw), BF16),
            pltpu.VMEM((ts, mw), BF16),
            pltpu.VMEM((ts, mw), BF16),
            pltpu.VMEM((ts, mw), BF16),
            pltpu.VMEM((ts, mw + gwv), F32),
            pltpu.VMEM((ts, mw + gwv), F32),
            col_f32,
            col_f32,
            pltpu.VMEM((LANES, ts), F32),
            col_f32,
            col_f32,
            pltpu.VMEM((max(nc, 8), LANES), F32),
            pltpu.VMEM((MLSTM_HEADS, hd, 2 * hd), F32),
            pltpu.VMEM((8, LANES), F32),
            pltpu.VMEM((gdv, gwk), F32),
        ],
        compiler_params=pltpu.CompilerParams(
            dimension_semantics=("arbitrary",), vmem_limit_bytes=VMEM_LIMIT_BYTES),
        name="mixer",
    )(x, x, g, w_cat, sbias, conv_w, conv_b, wa2_pad, ba, mng, gng, w_out)


def _attn_mlp_kernel(x_ref, k_ref, v_ref, xg_ref, wq_ref, wo_ref, mg_ref, w1_ref, w2_ref, fg_ref,
                     o_ref, *, ff_chunk):
    x = x_ref[...]
    D = x.shape[1]
    hd = D // XATTN_HEADS
    hq = _rmsnorm(x, xg_ref[...]).astype(BF16)
    q = (_dot(hq, wq_ref[...]) * (hd ** -0.5)).astype(BF16)
    atts = []
    for hh in range(XATTN_HEADS):
        cols = slice(hh * hd, (hh + 1) * hd)
        s = _dot_nt(q[:, cols], k_ref[0, :, cols])
        p = jnp.exp(s - jnp.max(s, axis=-1, keepdims=True))
        att = _dot(p.astype(BF16), v_ref[0, :, cols])
        atts.append((att / jnp.sum(p, axis=-1, keepdims=True)).astype(BF16))
    x = x + _dot(jnp.concatenate(atts, axis=1), wo_ref[...])

    hm = _rmsnorm(x, mg_ref[...]).astype(BF16)
    acc = x
    for c in range(w1_ref.shape[1] // ff_chunk):
        cols = slice(c * ff_chunk, (c + 1) * ff_chunk)
        t = jnp.maximum(_dot(hm, w1_ref[:, cols]), 0.0)
        acc = acc + _dot((t * t).astype(BF16), w2_ref[cols, :])
    o_ref[...] = _rmsnorm(acc, fg_ref[...])


def _attn_mlp(x2d, k, v, xg, wq, wo, mg, w1, w2, fg, *, tm, seq, ff_chunk):
    T, D = x2d.shape
    M = k.shape[1]
    F = w1.shape[1]
    assert T % tm == 0 and seq % tm == 0 and F % ff_chunk == 0
    per_b = seq // tm
    const = lambda i: (0, 0)
    single = pl.Buffered(1)
    kern = functools.partial(_attn_mlp_kernel, ff_chunk=ff_chunk)
    return pl.pallas_call(
        kern,
        grid=(T // tm,),
        in_specs=[
            pl.BlockSpec((tm, D), lambda i: (i, 0)),
            pl.BlockSpec((1, M, D), lambda i: (i // per_b, 0, 0)),
            pl.BlockSpec((1, M, D), lambda i: (i // per_b, 0, 0)),
            pl.BlockSpec((1, D), const),
            pl.BlockSpec((D, D), const, pipeline_mode=single),
            pl.BlockSpec((D, D), const, pipeline_mode=single),
            pl.BlockSpec((1, D), const),
            pl.BlockSpec((D, F), const, pipeline_mode=single),
            pl.BlockSpec((F, D), const, pipeline_mode=single),
            pl.BlockSpec((1, D), const),
        ],
        out_specs=pl.BlockSpec((tm, D), lambda i: (i, 0)),
        out_shape=jax.ShapeDtypeStruct((T, D), F32),
        compiler_params=pltpu.CompilerParams(
            dimension_semantics=("arbitrary",), vmem_limit_bytes=VMEM_LIMIT_BYTES),
        name="attn_mlp",
    )(x2d, k, v, xg, wq, wo, mg, w1, w2, fg)


MIXER_SEQ_TILE = 512
ATTN_MLP_TILE = 1024
FF_CHUNK = 1024


def kernel(x, mem, mix_norm_g, w_in, conv_w, conv_b, mlstm_i_b, mlstm_f_b, mlstm_norm_g, gla_wa2, gla_ba,
           gla_norm_g, w_out, xattn_norm_g, mem_norm_g, wq_x, wk_x, wv_x, wo_x, mlp_norm_g, w1, w2,
           final_norm_g):
    assert w_in.shape[0] == 1, "single-layer block"
    B, S, D = x.shape
    mw = mlstm_norm_g.shape[1]
    gwv = gla_norm_g.shape[1]
    rank, gwk = gla_wa2.shape[1:]
    nh = mlstm_i_b.shape[1]
    assert nh == MLSTM_HEADS and rank == GLA_RANK
    w_in = w_in[0]

    o_mi = 4 * mw
    o_gq = o_mi + 2 * nh
    o_ga = o_gq + 2 * gwk + 2 * gwv
    small_w = jnp.concatenate(
        [w_in[:, o_mi:o_gq], w_in[:, o_ga:o_ga + rank],
         jnp.zeros((D, LANES - 2 * nh - rank), w_in.dtype)], axis=1)
    w_cat = jnp.concatenate([w_in[:, 0:o_mi], w_in[:, o_gq:o_ga], small_w], axis=1).astype(BF16)
    sbias = jnp.concatenate([mlstm_i_b[0], mlstm_f_b[0], jnp.zeros((LANES - 2 * nh,), F32)])[None, :]
    wa2_pad = jnp.zeros((LANES, gwk), F32).at[2 * nh:2 * nh + rank].set(gla_wa2[0]).astype(BF16)

    k_mem, v_mem = _mem_kv(mem, mem_norm_g, wk_x[0].astype(BF16), wv_x[0].astype(BF16))
    x = _mixer(x, mix_norm_g, w_cat, sbias, conv_w[0], conv_b, wa2_pad, gla_ba,
               mlstm_norm_g, gla_norm_g, w_out[0].astype(BF16),
               ts=MIXER_SEQ_TILE, mw=mw, gwk=gwk, gwv=gwv)
    out = _attn_mlp(x.reshape(B * S, D), k_mem, v_mem, xattn_norm_g, wq_x[0].astype(BF16),
                    wo_x[0].astype(BF16), mlp_norm_g, w1[0].astype(BF16), w2[0].astype(BF16),
                    final_norm_g[None, :], tm=ATTN_MLP_TILE, seq=S, ff_chunk=FF_CHUNK)
    return out.reshape(B, S, D)
```

```python
import functools

import jax
import jax.numpy as jnp
from jax import lax
from jax.experimental import pallas as pl
from jax.experimental.pallas import tpu as pltpu

EPS = 1e-6
CHUNK = 64
CONV_W = 4
MLSTM_HEADS = 4
GLA_HEADS = 4
GLA_RANK = 16
GLA_TAU = 16.0
XATTN_HEADS = 4
LANES = 128
MXU_COLS = 256
VMEM_LIMIT_BYTES = 56 * 1024 * 1024

BF16 = jnp.bfloat16
F32 = jnp.float32


def _rmsnorm(x, g):
    return x * lax.rsqrt(jnp.mean(x * x, axis=-1, keepdims=True) + EPS) * g


def _log_sigmoid(z):
    return jnp.minimum(z, 0.0) - jnp.log(1.0 + jnp.exp(-jnp.abs(z)))


def _sigmoid(z):
    return 0.5 * jnp.tanh(0.5 * z) + 0.5


def _dot(a, b):
    return jnp.dot(a, b, preferred_element_type=F32)


def _dot_nt(a, b):
    return lax.dot_general(a, b, (((1,), (1,)), ((), ())), preferred_element_type=F32)


def _dot_tn(a, b):
    return lax.dot_general(a, b, (((0,), (0,)), ((), ())), preferred_element_type=F32)


def _segment_scan(x, seg, op, identity):
    row = lax.broadcasted_iota(jnp.int32, x.shape, 0) & (seg - 1)
    k = 1
    while k < seg:
        shifted = pltpu.roll(x, k, axis=0)
        x = op(x, jnp.where(row >= k, shifted, identity))
        k *= 2
    return x


def _mem_kv_kernel(mem_ref, g_ref, wk_ref, wv_ref, k_ref, v_ref):
    mn = _rmsnorm(mem_ref[0], g_ref[...]).astype(BF16)
    k_ref[0] = _dot(mn, wk_ref[...]).astype(BF16)
    v_ref[0] = _dot(mn, wv_ref[...]).astype(BF16)


def _mem_kv(mem, g, wk, wv):
    B, M, D = mem.shape
    const = lambda b: (0, 0)
    return pl.pallas_call(
        _mem_kv_kernel,
        grid=(B,),
        in_specs=[
            pl.BlockSpec((1, M, D), lambda b: (b, 0, 0)),
            pl.BlockSpec((1, D), const),
            pl.BlockSpec((D, D), const),
            pl.BlockSpec((D, D), const),
        ],
        out_specs=[
            pl.BlockSpec((1, M, D), lambda b: (b, 0, 0)),
            pl.BlockSpec((1, M, D), lambda b: (b, 0, 0)),
        ],
        out_shape=[jax.ShapeDtypeStruct((B, M, D), BF16)] * 2,
        compiler_params=pltpu.CompilerParams(
            dimension_semantics=("arbitrary",), vmem_limit_bytes=VMEM_LIMIT_BYTES),
        name="mem_kv",
    )(mem, g, wk, wv)


N_RAW = 9
GLA_SAFE_LOG_RANGE = 60.0


def _mixer_kernel(xa_ref, xb_ref, g_ref, w_ref, sbias_ref, convw_ref, convb_ref, wa2_ref, ba_ref,
                  mng_ref, gng_ref, wout_ref,
                  o_ref,
                  *scratch, ts, nt, mw, gwk, gwv):
    raw_sets = (scratch[0:N_RAW], scratch[N_RAW:2 * N_RAW])
    (flag_s, gintra_s, h_s, carry_s, q_s, qw_s, k_s, kw_s, gate_s, mix_s, b_s, cm_s, at_s, mx_s,
     emt_s, dec_s, c_s, m_s, st_s) = scratch[2 * N_RAW:]
    nc = ts // CHUNK
    hd = mw // MLSTM_HEADS
    gdk = gwk // GLA_HEADS
    gdv = gwv // GLA_HEADS
    o_v = 2 * mw
    o_o = 3 * mw
    o_gq = 4 * mw
    o_gk = o_gq + gwk
    o_gv = o_gk + gwk
    o_gr = o_gv + gwv
    o_sm = o_gr + gwv

    def normalise(x_ref, slot):
        h_s[slot] = _rmsnorm(x_ref[0], g_ref[...]).astype(BF16)

    def stage_a_pieces(raw, slot):
        uqk, v1, uo, ugq, ugk, ugv, ugr, usm, ubg = raw

        def proj(lo, hi):
            return _dot(h_s[slot], w_ref[:, lo:hi])

        def piece(lo, width, store):
            return lambda: store(proj(lo, lo + width))

        def store_to(ref, row0, c0, dtype):
            def store(u):
                ref[pl.ds(row0, ts), c0:c0 + u.shape[1]] = u.astype(dtype)
            return store

        def store_v(first_head):
            def store(u):
                for i in range(u.shape[1] // hd):
                    hh = first_head + i
                    v1[:, 2 * hh * hd:(2 * hh + 1) * hd] = u[:, i * hd:(i + 1) * hd].astype(BF16)
            return store

        w = MXU_COLS
        pieces = [piece(o_sm, LANES, store_to(usm, 0, 0, F32))]
        for c0 in range(0, 2 * mw, w):
            pieces.append(piece(c0, w, store_to(uqk, 8, c0, F32)))
        for c0 in range(0, mw, w):
            pieces.append(piece(o_v + c0, w, store_v(c0 // hd)))
        for c0 in range(0, mw, w):
            pieces.append(piece(o_o + c0, w, store_to(uo, 0, c0, F32)))
        for c0 in range(0, gwk, w):
            pieces.append(piece(o_gq + c0, w, store_to(ugq, 0, c0, F32)))
        for c0 in range(0, gwk, w):
            pieces.append(piece(o_gk + c0, w, store_to(ugk, 0, c0, F32)))
        for c0 in range(0, gwv, w):
            pieces.append(piece(o_gv + c0, w, store_to(ugv, 0, c0, BF16)))
        for c0 in range(0, gwv, w):
            pieces.append(piece(o_gr + c0, w, store_to(ugr, 0, c0, F32)))

        def p_bg():
            small = usm[...] + sbias_ref[...]
            la = _log_sigmoid(_dot(small.astype(BF16), wa2_ref[...]) + ba_ref[...]) * (1.0 / GLA_TAU)
            bg = _segment_scan(la, CHUNK, jnp.add, 0.0)
            ubg[...] = bg
            flag_s[slot] = (jnp.min(bg) < -GLA_SAFE_LOG_RANGE).astype(jnp.int32)

        half = (len(pieces) + 1) // 2
        return pieces[:half] + [p_bg] + pieces[half:]

    def gla_intra_exact(raw):
        ugq, ugk, ugv, ubg = raw[3], raw[4], raw[5], raw[8]
        s_idx = lax.broadcasted_iota(jnp.int32, (CHUNK, gwk), 0)
        t_idx = lax.broadcasted_iota(jnp.int32, (CHUNK, gwv), 0)
        ind =(lax.broadcasted_iota(jnp.int32, (gwk, LANES), 0) // gdk
               == lax.broadcasted_iota(jnp.int32, (gwk, LANES), 1)).astype(BF16)

        def chunk_body(c, carry):
            r0 = pl.multiple_of(c * CHUNK, CHUNK)
            bgc = ubg[pl.ds(r0, CHUNK), :]
            gkc = ugk[pl.ds(r0, CHUNK), :]
            gvc = ugv[pl.ds(r0, CHUNK), :]

            def row_body(t, acc):
                bgt = ubg[pl.ds(r0 + t, 1), :]
                gqt = ugq[pl.ds(r0 + t, 1), :] * (gdk ** -0.5)
                valid = s_idx <= t
                e = jnp.exp(jnp.where(valid, bgt - bgc, 0.0))
                g = jnp.where(valid, gqt * gkc * e, 0.0).astype(BF16)
                a_cols = _dot(g, ind).astype(BF16)
                res = _dot_tn(a_cols, gvc)
                row = jnp.concatenate(
                    [res[hh:hh + 1, hh * gdv:(hh + 1) * gdv] for hh in range(GLA_HEADS)], axis=1)
                return jnp.where(t_idx == t, row, acc)

            gintra_s[pl.ds(r0, CHUNK), :] = lax.fori_loop(
                0, CHUNK, row_body, jnp.zeros((CHUNK, gwv), F32))
            return carry

        lax.fori_loop(0, nc, chunk_body, 0)

    def stage_b(x_ref, raw, exact_intra, a_pieces, normalise_next):
        a_pieces = list(a_pieces)
        for _ in range(len(a_pieces) // 2):
            a_pieces.pop(0)()
        uqk, v1_s, uo, gq_s, gk_s, gv_s, ugr, usm, bg_s = raw
        gate_s[:, 0:mw] = _sigmoid(uo[...])
        gr = ugr[...]
        gate_s[:, mw:mw + gwv] = gr * _sigmoid(gr)
        small = usm[...] + sbias_ref[...]

        b = pltpu.roll(_segment_scan(_log_sigmoid(small), CHUNK, jnp.add, 0.0),
                       LANES - MLSTM_HEADS, axis=1)
        a = small - b
        cm = _segment_scan(a, CHUNK, jnp.maximum, -jnp.inf)
        b_s[...] = b
        cm_s[...] = cm
        b_last = b_s[pl.ds(CHUNK - 1, nc, stride=CHUNK), :]
        cm_last = cm_s[pl.ds(CHUNK - 1, nc, stride=CHUNK), :]
        m = m_s[0:1, :]
        m_rows = []
        lw_rows = []
        for c in range(nc):
            bl = b_last[c:c + 1, :]
            gmax = bl + cm_last[c:c + 1, :]
            m_new = jnp.maximum(bl + m, gmax)
            dec_s[c:c + 1, :] = jnp.exp(bl + m - m_new)
            m_rows.append(jnp.broadcast_to(m, (CHUNK, LANES)))
            lw_rows.append(jnp.broadcast_to(gmax - m_new - cm_last[c:c + 1, :], (CHUNK, LANES)))
            m = m_new
        m_s[0:1, :] = m
        mc = jnp.concatenate(m_rows, axis=0)
        mx = jnp.maximum(cm, mc)
        mx_s[...] = mx
        emt_s[...] = jnp.exp(-(b + mx))
        at_s[...] = a.T
        wint = jnp.exp(mc - mx)
        wkey = jnp.exp(a + jnp.concatenate(lw_rows, axis=0))

        uqk[0:8, :] = carry_s[...]
        y = convb_ref[...] + uqk[pl.ds(8, ts), :] * convw_ref[3:4, :]
        y = y + uqk[pl.ds(7, ts), :] * convw_ref[2:3, :]
        y = y + uqk[pl.ds(6, ts), :] * convw_ref[1:2, :]
        y = y + uqk[pl.ds(5, ts), :] * convw_ref[0:1, :]
        carry_s[...] = uqk[pl.ds(ts, 8), :]
        y = y * _sigmoid(y)
        for hh in range(MLSTM_HEADS):
            cols = slice(hh * hd, (hh + 1) * hd)
            qf = y[:, hh * hd:(hh + 1) * hd] * (hd ** -0.5)
            kf = y[:, mw + hh * hd:mw + (hh + 1) * hd]
            q_s[:, cols] = qf.astype(BF16)
            qw_s[:, cols] = (qf * wint[:, hh:hh + 1]).astype(BF16)
            k_s[:, cols] = kf.astype(BF16)
            kw_s[:, cols] = (kf * wkey[:, hh:hh + 1]).astype(BF16)

        ti = lax.broadcasted_iota(jnp.int32, (CHUNK, CHUNK), 0)
        si = lax.broadcasted_iota(jnp.int32, (CHUNK, CHUNK), 1)
        causal = ti >= si
        lane = lax.broadcasted_iota(jnp.int32, (CHUNK, gwk), 1)

        def level1(c):
            rows = pl.ds(c * CHUNK, CHUNK)
            dec = dec_s[c:c + 1, :]
            qk = []
            rhs = []
            for hh in range(MLSTM_HEADS):
                cols = slice(hh * hd, (hh + 1) * hd)
                v1h = v1_s[rows, 2 * hh * hd:(2 * hh + 2) * hd]
                c_prev = c_s[hh]
                qk.append(_dot_nt(q_s[rows, cols], k_s[rows, cols]))
                rhs.append(jnp.concatenate([c_prev.astype(BF16), v1h], axis=0))
                c_s[hh] = dec[:, hh:hh + 1] * c_prev + _dot_tn(kw_s[rows, cols], v1h)
            bg = bg_s[rows, :]
            bg_last = bg[CHUNK - 1:CHUNK, :]
            gq = gq_s[rows, :]
            gk = gk_s[rows, :]
            gv = gv_s[rows, :]
            qe = gq * jnp.exp(bg) * (gdk ** -0.5)
            ke = (gk * jnp.exp(-bg)).astype(BF16)
            kw = gk * jnp.exp(bg_last - bg)
            st_prev = st_s[...]
            qe_heads = []
            kw_heads = []
            for hh in range(GLA_HEADS):
                hm = (lane >= hh * gdk) & (lane < (hh + 1) * gdk)
                qe_heads.append(jnp.where(hm, qe, 0.0).astype(BF16))
                kw_heads.append(jnp.where(hm, kw, 0.0).astype(BF16))
            qe_stack = jnp.concatenate(qe_heads, axis=0)
            kw_stack = jnp.concatenate(kw_heads, axis=0)
            v_stack = jnp.concatenate(
                [gv[:, hh * gdv:(hh + 1) * gdv] for hh in range(GLA_HEADS)], axis=0)
            a_all = _dot_nt(qe_stack, ke)
            inter_all = _dot_nt(qe_stack, st_prev.astype(BF16))
            st_s[...] = st_prev * jnp.exp(bg_last) + _dot_tn(v_stack, kw_stack)
            return qk, rhs, a_all, inter_all

        gains = jnp.concatenate([mng_ref[...], gng_ref[...]], axis=1)

        def finish_head(rows, cols, hv):
            normed = hv * lax.rsqrt(jnp.mean(hv * hv, axis=-1, keepdims=True) + EPS)
            mix_s[rows, cols] = (normed * gains[:, cols] * gate_s[rows, cols]).astype(BF16)

        def level2(c, qk, rhs, a_all, inter_all):
            rows = pl.ds(c * CHUNK, CHUNK)
            mxc = mx_s[rows, :]
            em = emt_s[rows, :]
            gv = gv_s[rows, :]
            for hh in range(MLSTM_HEADS):
                cols = slice(hh * hd, (hh + 1) * hd)
                col = slice(hh, hh + 1)
                a_row = at_s[hh:hh + 1, c * CHUNK:(c + 1) * CHUNK]
                pm = jnp.exp(jnp.where(causal, a_row - mxc[:, col], -jnp.inf))
                lhs = jnp.concatenate([qw_s[rows, cols], (qk[hh] * pm).astype(BF16)], axis=1)
                r = _dot(lhs, rhs[hh])
                finish_head(rows, cols,
                            r[:, 0:hd] / jnp.maximum(jnp.abs(r[:, hd:2 * hd]), em[:, col]))
            for hh in range(GLA_HEADS):
                hrows = slice(hh * CHUNK, (hh + 1) * CHUNK)
                gcols = slice(hh * gdv, (hh + 1) * gdv)
                a_h = jnp.where(causal, a_all[hrows, :], 0.0).astype(BF16)
                intra = jnp.where(exact_intra, gintra_s[rows, gcols], _dot(a_h, gv[:, gcols]))
                finish_head(rows, slice(mw + hh * gdv, mw + (hh + 1) * gdv),
                            intra + inter_all[hrows, :])

        def next_a_piece():
            if a_pieces:
                a_pieces.pop(0)()

        keep_for_tail = 2
        pending = level1(0)
        for c in range(nc):
            current = pending
            if c + 1 < nc:
                pending = level1(c + 1)
                if len(a_pieces) > keep_for_tail:
                    next_a_piece()
            level2(c, *current)
            if c == nc // 4:
                normalise_next()

        while a_pieces:
            a_pieces.pop(0)()

        o_ref[0] = x_ref[0] + _dot(mix_s[...], wout_ref[...])

    j = pl.program_id(0)

    @pl.when(j == 0)
    def _():
        for raw in raw_sets:
            for hh in range(MLSTM_HEADS):
                raw[1][:, (2 * hh + 1) * hd:(2 * hh + 2) * hd] = jnp.ones((ts, hd), BF16)
        gintra_s[...] = jnp.zeros(gintra_s.shape, F32)
        flag_s[1] = jnp.int32(0)
        normalise(xb_ref, 0)
        for piece in stage_a_pieces(raw_sets[0], 0):
            piece()
        normalise(xa_ref, 1)

    @pl.when((j > 0) & ((j - 1) % nt == 0))
    def _():
        carry_s[...] = jnp.zeros(carry_s.shape, F32)
        c_s[...] = jnp.zeros(c_s.shape, F32)
        m_s[...] = jnp.zeros(m_s.shape, F32)
        st_s[...] = jnp.zeros(st_s.shape, F32)

    for parity in (0, 1):
        prev = 1 - parity

        @pl.when((j > 0) & (j % 2 == parity) & (flag_s[prev] != 0))
        def _():
            gla_intra_exact(raw_sets[prev])

        @pl.when((j > 0) & (j % 2 == parity))
        def _():
            stage_b(xb_ref, raw_sets[prev], flag_s[prev] != 0,
                    stage_a_pieces(raw_sets[parity], parity),
                    functools.partial(normalise, xa_ref, prev))


def _mixer(x, g, w_cat, sbias, conv_w, conv_b, wa2_pad, ba, mng, gng, w_out, *, ts, mw, gwk, gwv):
    B, S, D = x.shape
    ncol = w_cat.shape[1]
    nc = ts // CHUNK
    nt = S // ts
    n_tiles = B * nt
    hd = mw // MLSTM_HEADS
    gdv = gwv // GLA_HEADS
    assert gdv == hd and hd == LANES and gwk % LANES == 0 and S % ts == 0 and ts % (2 * CHUNK) == 0
    const = lambda j: (0, 0)

    def tile_a(j):
        t = jnp.minimum(j + 1, n_tiles - 1)
        return (t // nt, t % nt, 0)

    def tile_b(j):
        t = jnp.maximum(j - 1, 0)
        return (t // nt, t % nt, 0)

    single = pl.Buffered(1)
    col_f32 = pltpu.VMEM((ts, LANES), F32)
    raw_set = [
        pltpu.VMEM((ts + 8, 2 * mw), F32),
        pltpu.VMEM((ts, 2 * mw), BF16),
        pltpu.VMEM((ts, mw), F32),
        pltpu.VMEM((ts, gwk), F32),
        pltpu.VMEM((ts, gwk), F32),
        pltpu.VMEM((ts, gwv), BF16),
        pltpu.VMEM((ts, gwv), F32),
        col_f32,
        pltpu.VMEM((ts, gwk), F32),
    ]
    assert len(raw_set) == N_RAW
    kern = functools.partial(_mixer_kernel, ts=ts, nt=nt, mw=mw, gwk=gwk, gwv=gwv)
    return pl.pallas_call(
        kern,
        grid=(n_tiles + 1,),
        in_specs=[
            pl.BlockSpec((1, ts, D), tile_a),
            pl.BlockSpec((1, ts, D), tile_b),
            pl.BlockSpec((1, D), const),
            pl.BlockSpec((D, ncol), const, pipeline_mode=single),
            pl.BlockSpec((1, LANES), const),
            pl.BlockSpec((CONV_W, 2 * mw), const),
            pl.BlockSpec((1, 2 * mw), const),
            pl.BlockSpec((LANES, gwk), const),
            pl.BlockSpec((1, gwk), const),
            pl.BlockSpec((1, mw), const),
            pl.BlockSpec((1, gwv), const),
            pl.BlockSpec((mw + gwv, D), const, pipeline_mode=single),
        ],
        out_specs=pl.BlockSpec((1, ts, D), tile_b),
        out_shape=jax.ShapeDtypeStruct((B, S, D), F32),
        scratch_shapes=raw_set + raw_set + [
            pltpu.SMEM((2,), jnp.int32),
            pltpu.VMEM((ts, gwv), F32),
            pltpu.VMEM((2, ts, D), BF16),
            pltpu.VMEM((8, 2 * mw), F32),
            pltpu.VMEM((ts, mw), BF16),
            pltpu.VMEM((ts, mw), BF16),
            pltpu.VMEM((ts, mw), BF16),
            pltpu.VMEM((ts, mw), BF16),
            pltpu.VMEM((ts, mw + gwv), F32),
            pltpu.VMEM((ts, mw + gwv), BF16),
            col_f32,
            col_f32,
            pltpu.VMEM((LANES, ts), F32),
            col_f32,
            col_f32,
            pltpu.VMEM((max(nc, 8), LANES), F32),
            pltpu.VMEM((MLSTM_HEADS, hd, 2 * hd), F32),
            pltpu.VMEM((8, LANES), F32),
            pltpu.VMEM((gdv, gwk), F32),
        ],
        compiler_params=pltpu.CompilerParams(
            dimension_semantics=("arbitrary",), vmem_limit_bytes=VMEM_LIMIT_BYTES),
        name="mixer",
    )(x, x, g, w_cat, sbias, conv_w, conv_b, wa2_pad, ba, mng, gng, w_out)


def _attn_mlp_kernel(x_ref, k_ref, v_ref, xg_ref, wq_ref, wo_ref, mg_ref, w1_ref, w2_ref, fg_ref,
                     o_ref, *, ff_chunk):
    x = x_ref[...]
    D = x.shape[1]
    hd = D // XATTN_HEADS
    hq = _rmsnorm(x, xg_ref[...]).astype(BF16)
    q = (_dot(hq, wq_ref[...]) * (hd ** -0.5)).astype(BF16)
    atts = []
    for hh in range(XATTN_HEADS):
        cols = slice(hh * hd, (hh + 1) * hd)
        s = _dot_nt(q[:, cols], k_ref[0, :, cols])
        p = jnp.exp(s - jnp.max(s, axis=-1, keepdims=True))
        att = _dot(p.astype(BF16), v_ref[0, :, cols])
        atts.append((att / jnp.sum(p, axis=-1, keepdims=True)).astype(BF16))
    x = x + _dot(jnp.concatenate(atts, axis=1), wo_ref[...])

    hm = _rmsnorm(x, mg_ref[...]).astype(BF16)
    acc = x
    for c in range(w1_ref.shape[1] // ff_chunk):
        cols = slice(c * ff_chunk, (c + 1) * ff_chunk)
        t = jnp.maximum(_dot(hm, w1_ref[:, cols]), 0.0)
        acc = acc + _dot((t * t).astype(BF16), w2_ref[cols, :])
    o_ref[...] = _rmsnorm(acc, fg_ref[...])


def _attn_mlp(x2d, k, v, xg, wq, wo, mg, w1, w2, fg, *, tm, seq, ff_chunk):
    T, D = x2d.shape
    M = k.shape[1]
    F = w1.shape[1]
    assert T % tm == 0 and seq % tm == 0 and F % ff_chunk == 0
    per_b = seq // tm
    const = lambda i: (0, 0)
    single = pl.Buffered(1)
    kern = functools.partial(_attn_mlp_kernel, ff_chunk=ff_chunk)
    return pl.pallas_call(
        kern,
        grid=(T // tm,),
        in_specs=[
            pl.BlockSpec((tm, D), lambda i: (i, 0)),
            pl.BlockSpec((1, M, D), lambda i: (i // per_b, 0, 0)),
            pl.BlockSpec((1, M, D), lambda i: (i // per_b, 0, 0)),
            pl.BlockSpec((1, D), const),
            pl.BlockSpec((D, D), const, pipeline_mode=single),
            pl.BlockSpec((D, D), const, pipeline_mode=single),
            pl.BlockSpec((1, D), const),
            pl.BlockSpec((D, F), const, pipeline_mode=single),
            pl.BlockSpec((F, D), const, pipeline_mode=single),
            pl.BlockSpec((1, D), const),
        ],
        out_specs=pl.BlockSpec((tm, D), lambda i: (i, 0)),
        out_shape=jax.ShapeDtypeStruct((T, D), F32),
        compiler_params=pltpu.CompilerParams(
            dimension_semantics=("arbitrary",), vmem_limit_bytes=VMEM_LIMIT_BYTES),
        name="attn_mlp",
    )(x2d, k, v, xg, wq, wo, mg, w1, w2, fg)


MIXER_SEQ_TILE = 512
ATTN_MLP_TILE = 1024
FF_CHUNK = 1024


def kernel(x, mem, mix_norm_g, w_in, conv_w, conv_b, mlstm_i_b, mlstm_f_b, mlstm_norm_g, gla_wa2, gla_ba,
           gla_norm_g, w_out, xattn_norm_g, mem_norm_g, wq_x, wk_x, wv_x, wo_x, mlp_norm_g, w1, w2,
           final_norm_g):
    assert w_in.shape[0] == 1, "single-layer block"
    B, S, D = x.shape
    mw = mlstm_norm_g.shape[1]
    gwv = gla_norm_g.shape[1]
    rank, gwk = gla_wa2.shape[1:]
    nh = mlstm_i_b.shape[1]
    assert nh == MLSTM_HEADS and rank == GLA_RANK
    w_in = w_in[0]

    o_mi = 4 * mw
    o_gq = o_mi + 2 * nh
    o_ga = o_gq + 2 * gwk + 2 * gwv
    small_w = jnp.concatenate(
        [w_in[:, o_mi:o_gq], w_in[:, o_ga:o_ga + rank],
         jnp.zeros((D, LANES - 2 * nh - rank), w_in.dtype)], axis=1)
    w_cat = jnp.concatenate([w_in[:, 0:o_mi], w_in[:, o_gq:o_ga], small_w], axis=1).astype(BF16)
    sbias = jnp.concatenate([mlstm_i_b[0], mlstm_f_b[0], jnp.zeros((LANES - 2 * nh,), F32)])[None, :]
    wa2_pad = jnp.zeros((LANES, gwk), F32).at[2 * nh:2 * nh + rank].set(gla_wa2[0]).astype(BF16)

    k_mem, v_mem = _mem_kv(mem, mem_norm_g, wk_x[0].astype(BF16), wv_x[0].astype(BF16))
    x = _mixer(x, mix_norm_g, w_cat, sbias, conv_w[0], conv_b, wa2_pad, gla_ba,
               mlstm_norm_g, gla_norm_g, w_out[0].astype(BF16),
               ts=MIXER_SEQ_TILE, mw=mw, gwk=gwk, gwv=gwv)
    out = _attn_mlp(x.reshape(B * S, D), k_mem, v_mem, xattn_norm_g, wq_x[0].astype(BF16),
                    wo_x[0].astype(BF16), mlp_norm_g, w1[0].astype(BF16), w2[0].astype(BF16),
                    final_norm_g[None, :], tm=ATTN_MLP_TILE, seq=S, ff_chunk=FF_CHUNK)
    return out.reshape(B, S, D)
```

```python
import functools

import jax
import jax.numpy as jnp
from jax import lax
from jax.experimental import pallas as pl
from jax.experimental.pallas import tpu as pltpu

EPS = 1e-6
CHUNK = 64
CONV_W = 4
MLSTM_HEADS = 4
GLA_HEADS = 4
GLA_RANK = 16
GLA_TAU = 16.0
XATTN_HEADS = 4
LANES = 128
MXU_COLS = 256
VMEM_LIMIT_BYTES = 56 * 1024 * 1024

BF16 = jnp.bfloat16
F32 = jnp.float32


def _rmsnorm(x, g):
    return x * lax.rsqrt(jnp.mean(x * x, axis=-1, keepdims=True) + EPS) * g


def _log_sigmoid(z):
    return jnp.minimum(z, 0.0) - jnp.log(1.0 + jnp.exp(-jnp.abs(z)))


def _sigmoid(z):
    return 0.5 * jnp.tanh(0.5 * z) + 0.5


def _silu(z):
    t = 0.5 * z
    return t + t * jnp.tanh(t)


def _dot(a, b):
    return jnp.dot(a, b, preferred_element_type=F32)


def _dot_nt(a, b):
    return lax.dot_general(a, b, (((1,), (1,)), ((), ())), preferred_element_type=F32)


def _dot_tn(a, b):
    return lax.dot_general(a, b, (((0,), (0,)), ((), ())), preferred_element_type=F32)


def _segment_scan(x, seg, op, identity):
    row = lax.broadcasted_iota(jnp.int32, x.shape, 0) & (seg - 1)
    k = 1
    while k < seg:
        shifted = pltpu.roll(x, k, axis=0)
        x = op(x, jnp.where(row >= k, shifted, identity))
        k *= 2
    return x


def _mem_kv_kernel(mem_ref, g_ref, wk_ref, wv_ref, k_ref, v_ref):
    mn = _rmsnorm(mem_ref[0], g_ref[...]).astype(BF16)
    k_ref[0] = _dot(mn, wk_ref[...]).astype(BF16)
    v_ref[0] = _dot(mn, wv_ref[...]).astype(BF16)


def _mem_kv(mem, g, wk, wv):
    B, M, D = mem.shape
    const = lambda b: (0, 0)
    return pl.pallas_call(
        _mem_kv_kernel,
        grid=(B,),
        in_specs=[
            pl.BlockSpec((1, M, D), lambda b: (b, 0, 0)),
            pl.BlockSpec((1, D), const),
            pl.BlockSpec((D, D), const),
            pl.BlockSpec((D, D), const),
        ],
        out_specs=[
            pl.BlockSpec((1, M, D), lambda b: (b, 0, 0)),
            pl.BlockSpec((1, M, D), lambda b: (b, 0, 0)),
        ],
        out_shape=[jax.ShapeDtypeStruct((B, M, D), BF16)] * 2,
        compiler_params=pltpu.CompilerParams(
            dimension_semantics=("arbitrary",), vmem_limit_bytes=VMEM_LIMIT_BYTES),
        name="mem_kv",
    )(mem, g, wk, wv)


N_RAW = 9
GLA_SAFE_LOG_RANGE = 60.0


def _mixer_kernel(xa_ref, xb_ref, g_ref, w_ref, sbias_ref, convw_ref, convb_ref, wa2_ref, ba_ref,
                  mng_ref, gng_ref, wout_ref,
                  o_ref,
                  *scratch, ts, nt, mw, gwk, gwv):
    raw_sets = (scratch[0:N_RAW], scratch[N_RAW:2 * N_RAW])
    (flag_s, gintra_s, h_s, carry_s, q_s, qw_s, k_s, kw_s, gate_s, mix_s, b_s, cm_s, at_s, mx_s,
     emt_s, dec_s, c_s, m_s, st_s) = scratch[2 * N_RAW:]
    nc = ts // CHUNK
    hd = mw // MLSTM_HEADS
    gdk = gwk // GLA_HEADS
    gdv = gwv // GLA_HEADS
    o_v = 2 * mw
    o_o = 3 * mw
    o_gq = 4 * mw
    o_gk = o_gq + gwk
    o_gv = o_gk + gwk
    o_gr = o_gv + gwv
    o_sm = o_gr + gwv

    def normalise(x_ref, slot):
        h_s[slot] = _rmsnorm(x_ref[0], g_ref[...]).astype(BF16)

    def stage_a_pieces(raw, slot):
        uqk, v1, uo, ugq, ugk, ugv, ugr, usm, ubg = raw

        def proj(lo, hi):
            return _dot(h_s[slot], w_ref[:, lo:hi])

        def piece(lo, width, store):
            return lambda: store(proj(lo, lo + width))

        def store_to(ref, row0, c0, dtype):
            def store(u):
                ref[pl.ds(row0, ts), c0:c0 + u.shape[1]] = u.astype(dtype)
            return store

        def store_v(first_head):
            def store(u):
                for i in range(u.shape[1] // hd):
                    hh = first_head + i
                    v1[:, 2 * hh * hd:(2 * hh + 1) * hd] = u[:, i * hd:(i + 1) * hd].astype(BF16)
            return store

        w = MXU_COLS
        pieces = [piece(o_sm, LANES, store_to(usm, 0, 0, F32))]
        for c0 in range(0, 2 * mw, w):
            pieces.append(piece(c0, w, store_to(uqk, 8, c0, F32)))
        for c0 in range(0, mw, w):
            pieces.append(piece(o_v + c0, w, store_v(c0 // hd)))
        for c0 in range(0, mw, w):
            pieces.append(piece(o_o + c0, w, store_to(uo, 0, c0, F32)))
        for c0 in range(0, gwk, w):
            pieces.append(piece(o_gq + c0, w, store_to(ugq, 0, c0, F32)))
        for c0 in range(0, gwk, w):
            pieces.append(piece(o_gk + c0, w, store_to(ugk, 0, c0, F32)))
        for c0 in range(0, gwv, w):
            pieces.append(piece(o_gv + c0, w, store_to(ugv, 0, c0, BF16)))
        for c0 in range(0, gwv, w):
            pieces.append(piece(o_gr + c0, w, store_to(ugr, 0, c0, F32)))

        def p_bg():
            small = usm[...] + sbias_ref[...]
            la = _log_sigmoid(_dot(small.astype(BF16), wa2_ref[...]) + ba_ref[...]) * (1.0 / GLA_TAU)
            bg = _segment_scan(la, CHUNK, jnp.add, 0.0)
            ubg[...] = bg
            flag_s[slot] = (jnp.min(bg) < -GLA_SAFE_LOG_RANGE).astype(jnp.int32)

        half = (len(pieces) + 1) // 2
        return pieces[:half] + [p_bg] + pieces[half:]

    def gla_intra_exact(raw):
        ugq, ugk, ugv, ubg = raw[3], raw[4], raw[5], raw[8]
        s_idx = lax.broadcasted_iota(jnp.int32, (CHUNK, gwk), 0)
        t_idx = lax.broadcasted_iota(jnp.int32, (CHUNK, gwv), 0)
        ind =(lax.broadcasted_iota(jnp.int32, (gwk, LANES), 0) // gdk
               == lax.broadcasted_iota(jnp.int32, (gwk, LANES), 1)).astype(BF16)

        def chunk_body(c, carry):
            r0 = pl.multiple_of(c * CHUNK, CHUNK)
            bgc = ubg[pl.ds(r0, CHUNK), :]
            gkc = ugk[pl.ds(r0, CHUNK), :]
            gvc = ugv[pl.ds(r0, CHUNK), :]

            def row_body(t, acc):
                bgt = ubg[pl.ds(r0 + t, 1), :]
                gqt = ugq[pl.ds(r0 + t, 1), :] * (gdk ** -0.5)
                valid = s_idx <= t
                e = jnp.exp(jnp.where(valid, bgt - bgc, 0.0))
                g = jnp.where(valid, gqt * gkc * e, 0.0).astype(BF16)
                a_cols = _dot(g, ind).astype(BF16)
                res = _dot_tn(a_cols, gvc)
                row = jnp.concatenate(
                    [res[hh:hh + 1, hh * gdv:(hh + 1) * gdv] for hh in range(GLA_HEADS)], axis=1)
                return jnp.where(t_idx == t, row, acc)

            gintra_s[pl.ds(r0, CHUNK), :] = lax.fori_loop(
                0, CHUNK, row_body, jnp.zeros((CHUNK, gwv), F32))
            return carry

        lax.fori_loop(0, nc, chunk_body, 0)

    def stage_b(x_ref, raw, exact_intra, a_pieces, normalise_next):
        a_pieces = list(a_pieces)
        for _ in range(len(a_pieces) // 2):
            a_pieces.pop(0)()
        uqk, v1_s, uo, gq_s, gk_s, gv_s, ugr, usm, bg_s = raw
        gate_s[:, 0:mw] = _sigmoid(uo[...])
        gr = ugr[...]
        gate_s[:, mw:mw + gwv] = _silu(gr)
        small = usm[...] + sbias_ref[...]

        b = pltpu.roll(_segment_scan(_log_sigmoid(small), CHUNK, jnp.add, 0.0),
                       LANES - MLSTM_HEADS, axis=1)
        a = small - b
        cm = _segment_scan(a, CHUNK, jnp.maximum, -jnp.inf)
        b_s[...] = b
        cm_s[...] = cm
        b_last = b_s[pl.ds(CHUNK - 1, nc, stride=CHUNK), :]
        cm_last = cm_s[pl.ds(CHUNK - 1, nc, stride=CHUNK), :]
        m = m_s[0:1, :]
        m_rows = []
        lw_rows = []
        for c in range(nc):
            bl = b_last[c:c + 1, :]
            gmax = bl + cm_last[c:c + 1, :]
            m_new = jnp.maximum(bl + m, gmax)
            dec_s[c:c + 1, :] = jnp.exp(bl + m - m_new)
            m_rows.append(jnp.broadcast_to(m, (CHUNK, LANES)))
            lw_rows.append(jnp.broadcast_to(gmax - m_new - cm_last[c:c + 1, :], (CHUNK, LANES)))
            m = m_new
        m_s[0:1, :] = m
        mc = jnp.concatenate(m_rows, axis=0)
        mx = jnp.maximum(cm, mc)
        mx_s[...] = mx
        emt_s[...] = jnp.exp(-(b + mx))
        at_s[...] = a.T
        wint = jnp.exp(mc - mx)
        wkey = jnp.exp(a + jnp.concatenate(lw_rows, axis=0))

        uqk[0:8, :] = carry_s[...]
        y = convb_ref[...] + uqk[pl.ds(8, ts), :] * convw_ref[3:4, :]
        y = y + uqk[pl.ds(7, ts), :] * convw_ref[2:3, :]
        y = y + uqk[pl.ds(6, ts), :] * convw_ref[1:2, :]
        y = y + uqk[pl.ds(5, ts), :] * convw_ref[0:1, :]
        carry_s[...] = uqk[pl.ds(ts, 8), :]
        y = _silu(y)
        for hh in range(MLSTM_HEADS):
            cols = slice(hh * hd, (hh + 1) * hd)
            qf = y[:, hh * hd:(hh + 1) * hd] * (hd ** -0.5)
            kf = y[:, mw + hh * hd:mw + (hh + 1) * hd]
            q_s[:, cols] = qf.astype(BF16)
            qw_s[:, cols] = (qf * wint[:, hh:hh + 1]).astype(BF16)
            k_s[:, cols] = kf.astype(BF16)
            kw_s[:, cols] = (kf * wkey[:, hh:hh + 1]).astype(BF16)

        ti = lax.broadcasted_iota(jnp.int32, (CHUNK, CHUNK), 0)
        si = lax.broadcasted_iota(jnp.int32, (CHUNK, CHUNK), 1)
        causal = ti >= si
        lane = lax.broadcasted_iota(jnp.int32, (CHUNK, gwk), 1)

        def level1(c):
            rows = pl.ds(c * CHUNK, CHUNK)
            dec = dec_s[c:c + 1, :]
            qk = []
            rhs = []
            for hh in range(MLSTM_HEADS):
                cols = slice(hh * hd, (hh + 1) * hd)
                v1h = v1_s[rows, 2 * hh * hd:(2 * hh + 2) * hd]
                c_prev = c_s[hh]
                qk.append(_dot_nt(q_s[rows, cols], k_s[rows, cols]))
                rhs.append(jnp.concatenate([c_prev.astype(BF16), v1h], axis=0))
                c_s[hh] = dec[:, hh:hh + 1] * c_prev + _dot_tn(kw_s[rows, cols], v1h)
            bg = bg_s[rows, :]
            bg_last = bg[CHUNK - 1:CHUNK, :]
            gq = gq_s[rows, :]
            gk = gk_s[rows, :]
            gv = gv_s[rows, :]
            qe = gq * jnp.exp(bg) * (gdk ** -0.5)
            ke = (gk * jnp.exp(-bg)).astype(BF16)
            kw = gk * jnp.exp(bg_last - bg)
            st_prev = st_s[...]
            qe_heads = []
            kw_heads = []
            for hh in range(GLA_HEADS):
                hm = (lane >= hh * gdk) & (lane < (hh + 1) * gdk)
                qe_heads.append(jnp.where(hm, qe, 0.0).astype(BF16))
                kw_heads.append(jnp.where(hm, kw, 0.0).astype(BF16))
            qe_stack = jnp.concatenate(qe_heads, axis=0)
            kw_stack = jnp.concatenate(kw_heads, axis=0)
            v_stack = jnp.concatenate(
                [gv[:, hh * gdv:(hh + 1) * gdv] for hh in range(GLA_HEADS)], axis=0)
            a_all = _dot_nt(qe_stack, ke)
            inter_all = _dot_nt(qe_stack, st_prev.astype(BF16))
            st_s[...] = st_prev * jnp.exp(bg_last) + _dot_tn(v_stack, kw_stack)
            return qk, rhs, a_all, inter_all

        gains = jnp.concatenate([mng_ref[...], gng_ref[...]], axis=1)

        def finish_head(rows, cols, hv):
            normed = hv * lax.rsqrt(jnp.mean(hv * hv, axis=-1, keepdims=True) + EPS)
            mix_s[rows, cols] = (normed * gains[:, cols] * gate_s[rows, cols]).astype(BF16)

        def level2(c, qk, rhs, a_all, inter_all):
            rows = pl.ds(c * CHUNK, CHUNK)
            mxc = mx_s[rows, :]
            em = emt_s[rows, :]
            gv = gv_s[rows, :]
            for hh in range(MLSTM_HEADS):
                cols = slice(hh * hd, (hh + 1) * hd)
                col = slice(hh, hh + 1)
                a_row = at_s[hh:hh + 1, c * CHUNK:(c + 1) * CHUNK]
                pm = jnp.exp(jnp.where(causal, a_row - mxc[:, col], -jnp.inf))
                lhs = jnp.concatenate([qw_s[rows, cols], (qk[hh] * pm).astype(BF16)], axis=1)
                r = _dot(lhs, rhs[hh])
                finish_head(rows, cols,
                            r[:, 0:hd] / jnp.maximum(jnp.abs(r[:, hd:2 * hd]), em[:, col]))
            for hh in range(GLA_HEADS):
                hrows = slice(hh * CHUNK, (hh + 1) * CHUNK)
                gcols = slice(hh * gdv, (hh + 1) * gdv)
                a_h = jnp.where(causal, a_all[hrows, :], 0.0).astype(BF16)
                intra = jnp.where(exact_intra, gintra_s[rows, gcols], _dot(a_h, gv[:, gcols]))
                finish_head(rows, slice(mw + hh * gdv, mw + (hh + 1) * gdv),
                            intra + inter_all[hrows, :])

        def next_a_piece():
            if a_pieces:
                a_pieces.pop(0)()

        def out_proj(r0, r1):
            o_ref[0, r0:r1, :] = x_ref[0, r0:r1, :] + _dot(mix_s[r0:r1, :], wout_ref[...])

        keep_for_tail = 2
        pending = level1(0)
        for c in range(nc):
            current = pending
            if c + 1 < nc:
                pending = level1(c + 1)
                if len(a_pieces) > keep_for_tail:
                    next_a_piece()
            if c == nc - 1:
                out_proj(0, ts // 2)
            level2(c, *current)
            if c == nc // 4:
                normalise_next()

        while a_pieces:
            a_pieces.pop(0)()
        out_proj(ts // 2, ts)

    j = pl.program_id(0)

    @pl.when(j == 0)
    def _():
        for raw in raw_sets:
            for hh in range(MLSTM_HEADS):
                raw[1][:, (2 * hh + 1) * hd:(2 * hh + 2) * hd] = jnp.ones((ts, hd), BF16)
        gintra_s[...] = jnp.zeros(gintra_s.shape, F32)
        flag_s[1] = jnp.int32(0)
        normalise(xb_ref, 0)
        for piece in stage_a_pieces(raw_sets[0], 0):
            piece()
        normalise(xa_ref, 1)

    @pl.when((j > 0) & ((j - 1) % nt == 0))
    def _():
        carry_s[...] = jnp.zeros(carry_s.shape, F32)
        c_s[...] = jnp.zeros(c_s.shape, F32)
        m_s[...] = jnp.zeros(m_s.shape, F32)
        st_s[...] = jnp.zeros(st_s.shape, F32)

    for parity in (0, 1):
        prev = 1 - parity

        @pl.when((j > 0) & (j % 2 == parity) & (flag_s[prev] != 0))
        def _():
            gla_intra_exact(raw_sets[prev])

        @pl.when((j > 0) & (j % 2 == parity))
        def _():
            stage_b(xb_ref, raw_sets[prev], flag_s[prev] != 0,
                    stage_a_pieces(raw_sets[parity], parity),
                    functools.partial(normalise, xa_ref, prev))


def _mixer(x, g, w_cat, sbias, conv_w, conv_b, wa2_pad, ba, mng, gng, w_out, *, ts, mw, gwk, gwv):
    B, S, D = x.shape
    ncol = w_cat.shape[1]
    nc = ts // CHUNK
    nt = S // ts
    n_tiles = B * nt
    hd = mw // MLSTM_HEADS
    gdv = gwv // GLA_HEADS
    assert gdv == hd and hd == LANES and gwk % LANES == 0 and S % ts == 0 and ts % (2 * CHUNK) == 0
    const = lambda j: (0, 0)

    def tile_a(j):
        t = jnp.minimum(j + 1, n_tiles - 1)
        return (t // nt, t % nt, 0)

    def tile_b(j):
        t = jnp.maximum(j - 1, 0)
        return (t // nt, t % nt, 0)

    single = pl.Buffered(1)
    col_f32 = pltpu.VMEM((ts, LANES), F32)
    raw_set = [
        pltpu.VMEM((ts + 8, 2 * mw), F32),
        pltpu.VMEM((ts, 2 * mw), BF16),
        pltpu.VMEM((ts, mw), F32),
        pltpu.VMEM((ts, gwk), F32),
        pltpu.VMEM((ts, gwk), F32),
        pltpu.VMEM((ts, gwv), BF16),
        pltpu.VMEM((ts, gwv), F32),
        col_f32,
        pltpu.VMEM((ts, gwk), F32),
    ]
    assert len(raw_set) == N_RAW
    kern = functools.partial(_mixer_kernel, ts=ts, nt=nt, mw=mw, gwk=gwk, gwv=gwv)
    return pl.pallas_call(
        kern,
        grid=(n_tiles + 1,),
        in_specs=[
            pl.BlockSpec((1, ts, D), tile_a),
            pl.BlockSpec((1, ts, D), tile_b),
            pl.BlockSpec((1, D), const),
            pl.BlockSpec((D, ncol), const, pipeline_mode=single),
            pl.BlockSpec((1, LANES), const),
            pl.BlockSpec((CONV_W, 2 * mw), const),
            pl.BlockSpec((1, 2 * mw), const),
            pl.BlockSpec((LANES, gwk), const),
            pl.BlockSpec((1, gwk), const),
            pl.BlockSpec((1, mw), const),
            pl.BlockSpec((1, gwv), const),
            pl.BlockSpec((mw + gwv, D), const, pipeline_mode=single),
        ],
        out_specs=pl.BlockSpec((1, ts, D), tile_b),
        out_shape=jax.ShapeDtypeStruct((B, S, D), F32),
        scratch_shapes=raw_set + raw_set + [
            pltpu.SMEM((2,), jnp.int32),
            pltpu.VMEM((ts, gwv), F32),
            pltpu.VMEM((2, ts, D), BF16),
            pltpu.VMEM((8, 2 * mw), F32),
            pltpu.VMEM((ts, mw), BF16),
            pltpu.VMEM((ts, mw), BF16),
            pltpu.VMEM((ts, mw), BF16),
            pltpu.VMEM((ts, mw), BF16),
            pltpu.VMEM((ts, mw + gwv), F32),
            pltpu.VMEM((ts, mw + gwv), BF16),
            col_f32,
            col_f32,
            pltpu.VMEM((LANES, ts), F32),
            col_f32,
            col_f32,
            pltpu.VMEM((max(nc, 8), LANES), F32),
            pltpu.VMEM((MLSTM_HEADS, hd, 2 * hd), F32),
            pltpu.VMEM((8, LANES), F32),
            pltpu.VMEM((gdv, gwk), F32),
        ],
        compiler_params=pltpu.CompilerParams(
            dimension_semantics=("arbitrary",), vmem_limit_bytes=VMEM_LIMIT_BYTES),
        name="mixer",
    )(x, x, g, w_cat, sbias, conv_w, conv_b, wa2_pad, ba, mng, gng, w_out)


def _attn_mlp_kernel(x_ref, k_ref, v_ref, xg_ref, wq_ref, wo_ref, mg_ref, w1_ref, w2_ref, fg_ref,
                     o_ref, *, ff_chunk):
    x = x_ref[...]
    D = x.shape[1]
    hd = D // XATTN_HEADS
    hq = _rmsnorm(x, xg_ref[...]).astype(BF16)
    q = (_dot(hq, wq_ref[...]) * (hd ** -0.5)).astype(BF16)
    atts = []
    for hh in range(XATTN_HEADS):
        cols = slice(hh * hd, (hh + 1) * hd)
        s = _dot_nt(q[:, cols], k_ref[0, :, cols])
        p = jnp.exp(s - jnp.max(s, axis=-1, keepdims=True))
        att = _dot(p.astype(BF16), v_ref[0, :, cols])
        atts.append((att / jnp.sum(p, axis=-1, keepdims=True)).astype(BF16))
    x = x + _dot(jnp.concatenate(atts, axis=1), wo_ref[...])

    hm = _rmsnorm(x, mg_ref[...]).astype(BF16)
    acc = x
    for c in range(w1_ref.shape[1] // ff_chunk):
        cols = slice(c * ff_chunk, (c + 1) * ff_chunk)
        t = jnp.maximum(_dot(hm, w1_ref[:, cols]), 0.0)
        acc = acc + _dot((t * t).astype(BF16), w2_ref[cols, :])
    o_ref[...] = _rmsnorm(acc, fg_ref[...])


def _attn_mlp(x2d, k, v, xg, wq, wo, mg, w1, w2, fg, *, tm, seq, ff_chunk):
    T, D = x2d.shape
    M = k.shape[1]
    F = w1.shape[1]
    assert T % tm == 0 and seq % tm == 0 and F % ff_chunk == 0
    per_b = seq // tm
    const = lambda i: (0, 0)
    single = pl.Buffered(1)
    kern = functools.partial(_attn_mlp_kernel, ff_chunk=ff_chunk)
    return pl.pallas_call(
        kern,
        grid=(T // tm,),
        in_specs=[
            pl.BlockSpec((tm, D), lambda i: (i, 0)),
            pl.BlockSpec((1, M, D), lambda i: (i // per_b, 0, 0)),
            pl.BlockSpec((1, M, D), lambda i: (i // per_b, 0, 0)),
            pl.BlockSpec((1, D), const),
            pl.BlockSpec((D, D), const, pipeline_mode=single),
            pl.BlockSpec((D, D), const, pipeline_mode=single),
            pl.BlockSpec((1, D), const),
            pl.BlockSpec((D, F), const, pipeline_mode=single),
            pl.BlockSpec((F, D), const, pipeline_mode=single),
            pl.BlockSpec((1, D), const),
        ],
        out_specs=pl.BlockSpec((tm, D), lambda i: (i, 0)),
        out_shape=jax.ShapeDtypeStruct((T, D), F32),
        compiler_params=pltpu.CompilerParams(
            dimension_semantics=("arbitrary",), vmem_limit_bytes=VMEM_LIMIT_BYTES),
        name="attn_mlp",
    )(x2d, k, v, xg, wq, wo, mg, w1, w2, fg)


MIXER_SEQ_TILE = 512
ATTN_MLP_TILE = 1024
FF_CHUNK = 1024


def kernel(x, mem, mix_norm_g, w_in, conv_w, conv_b, mlstm_i_b, mlstm_f_b, mlstm_norm_g, gla_wa2, gla_ba,
           gla_norm_g, w_out, xattn_norm_g, mem_norm_g, wq_x, wk_x, wv_x, wo_x, mlp_norm_g, w1, w2,
           final_norm_g):
    assert w_in.shape[0] == 1, "single-layer block"
    B, S, D = x.shape
    mw = mlstm_norm_g.shape[1]
    gwv = gla_norm_g.shape[1]
    rank, gwk = gla_wa2.shape[1:]
    nh = mlstm_i_b.shape[1]
    assert nh == MLSTM_HEADS and rank == GLA_RANK
    w_in = w_in[0]

    o_mi = 4 * mw
    o_gq = o_mi + 2 * nh
    o_ga = o_gq + 2 * gwk + 2 * gwv
    small_w = jnp.concatenate(
        [w_in[:, o_mi:o_gq], w_in[:, o_ga:o_ga + rank],
         jnp.zeros((D, LANES - 2 * nh - rank), w_in.dtype)], axis=1)
    w_cat = jnp.concatenate([w_in[:, 0:o_mi], w_in[:, o_gq:o_ga], small_w], axis=1).astype(BF16)
    sbias = jnp.concatenate([mlstm_i_b[0], mlstm_f_b[0], jnp.zeros((LANES - 2 * nh,), F32)])[None, :]
    wa2_pad = jnp.zeros((LANES, gwk), F32).at[2 * nh:2 * nh + rank].set(gla_wa2[0]).astype(BF16)

    k_mem, v_mem = _mem_kv(mem, mem_norm_g, wk_x[0].astype(BF16), wv_x[0].astype(BF16))
    x = _mixer(x, mix_norm_g, w_cat, sbias, conv_w[0], conv_b, wa2_pad, gla_ba,
               mlstm_norm_g, gla_norm_g, w_out[0].astype(BF16),
               ts=MIXER_SEQ_TILE, mw=mw, gwk=gwk, gwv=gwv)
    out = _attn_mlp(x.reshape(B * S, D), k_mem, v_mem, xattn_norm_g, wq_x[0].astype(BF16),
                    wo_x[0].astype(BF16), mlp_norm_g, w1[0].astype(BF16), w2[0].astype(BF16),
                    final_norm_g[None, :], tm=ATTN_MLP_TILE, seq=S, ff_chunk=FF_CHUNK)
    return out.reshape(B, S, D)
```

```python
import functools

import jax
import jax.numpy as jnp
from jax import lax
from jax.experimental import pallas as pl
from jax.experimental.pallas import tpu as pltpu

EPS = 1e-6
CHUNK = 64
CONV_W = 4
MLSTM_HEADS = 4
GLA_HEADS = 4
GLA_RANK = 16
GLA_TAU = 16.0
XATTN_HEADS = 4
LANES = 128
MXU_COLS = 256
VMEM_LIMIT_BYTES = 56 * 1024 * 1024

BF16 = jnp.bfloat16
F32 = jnp.float32


def _rmsnorm(x, g):
    return x * lax.rsqrt(jnp.mean(x * x, axis=-1, keepdims=True) + EPS) * g


def _log_sigmoid(z):
    return jnp.minimum(z, 0.0) - jnp.log(1.0 + jnp.exp(-jnp.abs(z)))


def _sigmoid(z):
    return 0.5 * jnp.tanh(0.5 * z) + 0.5


def _dot(a, b):
    return jnp.dot(a, b, preferred_element_type=F32)


def _dot_nt(a, b):
    return lax.dot_general(a, b, (((1,), (1,)), ((), ())), preferred_element_type=F32)


def _dot_tn(a, b):
    return lax.dot_general(a, b, (((0,), (0,)), ((), ())), preferred_element_type=F32)


def _segment_scan(x, seg, op, identity):
    row = lax.broadcasted_iota(jnp.int32, x.shape, 0) & (seg - 1)
    k = 1
    while k < seg:
        shifted = pltpu.roll(x, k, axis=0)
        x = op(x, jnp.where(row >= k, shifted, identity))
        k *= 2
    return x


def _mem_kv_kernel(mem_ref, g_ref, wk_ref, wv_ref, k_ref, v_ref):
    mn = _rmsnorm(mem_ref[...], g_ref[...]).astype(BF16)
    k_ref[...] = _dot(mn, wk_ref[...]).astype(BF16)
    v_ref[...] = _dot(mn, wv_ref[...]).astype(BF16)


def _mem_kv(mem, g, wk, wv, *, rows):
    B, M, D = mem.shape
    assert (B * M) % rows == 0
    const = lambda i: (0, 0)
    tile = lambda i: (i, 0)
    k, v = pl.pallas_call(
        _mem_kv_kernel,
        grid=(B * M // rows,),
        in_specs=[
            pl.BlockSpec((rows, D), tile),
            pl.BlockSpec((1, D), const),
            pl.BlockSpec((D, D), const),
            pl.BlockSpec((D, D), const),
        ],
        out_specs=[pl.BlockSpec((rows, D), tile), pl.BlockSpec((rows, D), tile)],
        out_shape=[jax.ShapeDtypeStruct((B * M, D), BF16)] * 2,
        compiler_params=pltpu.CompilerParams(
            dimension_semantics=("arbitrary",), vmem_limit_bytes=VMEM_LIMIT_BYTES),
        name="mem_kv",
    )(mem.reshape(B * M, D), g, wk, wv)
    return k.reshape(B, M, D), v.reshape(B, M, D)


N_RAW = 9
GLA_SAFE_LOG_RANGE = 60.0


def _mixer_kernel(xa_ref, xb_ref, g_ref, w_ref, sbias_ref, convw_ref, convb_ref, wa2_ref, ba_ref,
                  mng_ref, gng_ref, wout_ref,
                  o_ref,
                  *scratch, ts, nt, mw, gwk, gwv):
    raw_sets = (scratch[0:N_RAW], scratch[N_RAW:2 * N_RAW])
    (flag_s, gintra_s, h_s, carry_s, q_s, qw_s, k_s, kw_s, gate_s, mix_s, b_s, cm_s, at_s, mx_s,
     emt_s, dec_s, c_s, m_s, st_s) = scratch[2 * N_RAW:]
    nc = ts // CHUNK
    hd = mw // MLSTM_HEADS
    gdk = gwk // GLA_HEADS
    gdv = gwv // GLA_HEADS
    o_v = 2 * mw
    o_o = 3 * mw
    o_gq = 4 * mw
    o_gk = o_gq + gwk
    o_gv = o_gk + gwk
    o_gr = o_gv + gwv
    o_sm = o_gr + gwv

    def normalise(x_ref, slot):
        h_s[slot] = _rmsnorm(x_ref[0], g_ref[...]).astype(BF16)

    def stage_a_pieces(raw, slot):
        uqk, v1, uo, ugq, ugk, ugv, ugr, usm, ubg = raw

        def proj(lo, hi):
            return _dot(h_s[slot], w_ref[:, lo:hi])

        def piece(lo, width, store):
            return lambda: store(proj(lo, lo + width))

        def store_to(ref, row0, c0, dtype):
            def store(u):
                ref[pl.ds(row0, ts), c0:c0 + u.shape[1]] = u.astype(dtype)
            return store

        def store_v(first_head):
            def store(u):
                for i in range(u.shape[1] // hd):
                    hh = first_head + i
                    v1[:, 2 * hh * hd:(2 * hh + 1) * hd] = u[:, i * hd:(i + 1) * hd].astype(BF16)
            return store

        w = MXU_COLS
        pieces = [piece(o_sm, LANES, store_to(usm, 0, 0, F32))]
        for c0 in range(0, 2 * mw, w):
            pieces.append(piece(c0, w, store_to(uqk, 8, c0, F32)))
        for c0 in range(0, mw, w):
            pieces.append(piece(o_v + c0, w, store_v(c0 // hd)))
        for c0 in range(0, mw, w):
            pieces.append(piece(o_o + c0, w, store_to(uo, 0, c0, F32)))
        for c0 in range(0, gwk, w):
            pieces.append(piece(o_gq + c0, w, store_to(ugq, 0, c0, F32)))
        for c0 in range(0, gwk, w):
            pieces.append(piece(o_gk + c0, w, store_to(ugk, 0, c0, F32)))
        for c0 in range(0, gwv, w):
            pieces.append(piece(o_gv + c0, w, store_to(ugv, 0, c0, BF16)))
        for c0 in range(0, gwv, w):
            pieces.append(piece(o_gr + c0, w, store_to(ugr, 0, c0, F32)))

        def p_bg():
            small = usm[...] + sbias_ref[...]
            la = _log_sigmoid(_dot(small.astype(BF16), wa2_ref[...]) + ba_ref[...]) * (1.0 / GLA_TAU)
            bg = _segment_scan(la, CHUNK, jnp.add, 0.0)
            ubg[...] = bg
            flag_s[slot] = (jnp.min(bg) < -GLA_SAFE_LOG_RANGE).astype(jnp.int32)

        half = (len(pieces) + 1) // 2
        return pieces[:half] + [p_bg] + pieces[half:]

    def gla_intra_exact(raw):
        ugq, ugk, ugv, ubg = raw[3], raw[4], raw[5], raw[8]
        s_idx = lax.broadcasted_iota(jnp.int32, (CHUNK, gwk), 0)
        t_idx = lax.broadcasted_iota(jnp.int32, (CHUNK, gwv), 0)
        ind =(lax.broadcasted_iota(jnp.int32, (gwk, LANES), 0) // gdk
               == lax.broadcasted_iota(jnp.int32, (gwk, LANES), 1)).astype(BF16)

        def chunk_body(c, carry):
            r0 = pl.multiple_of(c * CHUNK, CHUNK)
            bgc = ubg[pl.ds(r0, CHUNK), :]
            gkc = ugk[pl.ds(r0, CHUNK), :]
            gvc = ugv[pl.ds(r0, CHUNK), :]

            def row_body(t, acc):
                bgt = ubg[pl.ds(r0 + t, 1), :]
                gqt = ugq[pl.ds(r0 + t, 1), :] * (gdk ** -0.5)
                valid = s_idx <= t
                e = jnp.exp(jnp.where(valid, bgt - bgc, 0.0))
                g = jnp.where(valid, gqt * gkc * e, 0.0).astype(BF16)
                a_cols = _dot(g, ind).astype(BF16)
                res = _dot_tn(a_cols, gvc)
                row = jnp.concatenate(
                    [res[hh:hh + 1, hh * gdv:(hh + 1) * gdv] for hh in range(GLA_HEADS)], axis=1)
                return jnp.where(t_idx == t, row, acc)

            gintra_s[pl.ds(r0, CHUNK), :] = lax.fori_loop(
                0, CHUNK, row_body, jnp.zeros((CHUNK, gwv), F32))
            return carry

        lax.fori_loop(0, nc, chunk_body, 0)

    def stage_b(x_ref, raw, exact_intra, a_pieces, normalise_next):
        a_pieces = list(a_pieces)
        for _ in range(len(a_pieces) // 2):
            a_pieces.pop(0)()
        uqk, v1_s, uo, gq_s, gk_s, gv_s, ugr, usm, bg_s = raw
        gate_s[:, 0:mw] = _sigmoid(uo[...])
        gr = ugr[...]
        gate_s[:, mw:mw + gwv] = gr * _sigmoid(gr)
        small = usm[...] + sbias_ref[...]

        b = pltpu.roll(_segment_scan(_log_sigmoid(small), CHUNK, jnp.add, 0.0),
                       LANES - MLSTM_HEADS, axis=1)
        a = small - b
        cm = _segment_scan(a, CHUNK, jnp.maximum, -jnp.inf)
        b_s[...] = b
        cm_s[...] = cm
        b_last = b_s[pl.ds(CHUNK - 1, nc, stride=CHUNK), :]
        cm_last = cm_s[pl.ds(CHUNK - 1, nc, stride=CHUNK), :]
        m = m_s[0:1, :]
        m_rows = []
        lw_rows = []
        for c in range(nc):
            bl = b_last[c:c + 1, :]
            gmax = bl + cm_last[c:c + 1, :]
            m_new = jnp.maximum(bl + m, gmax)
            dec_s[c:c + 1, :] = jnp.exp(bl + m - m_new)
            m_rows.append(jnp.broadcast_to(m, (CHUNK, LANES)))
            lw_rows.append(jnp.broadcast_to(gmax - m_new - cm_last[c:c + 1, :], (CHUNK, LANES)))
            m = m_new
        m_s[0:1, :] = m
        mc = jnp.concatenate(m_rows, axis=0)
        mx = jnp.maximum(cm, mc)
        mx_s[...] = mx
        emt_s[...] = jnp.exp(-(b + mx))
        at_s[...] = a.T
        wint = jnp.exp(mc - mx)
        wkey = jnp.exp(a + jnp.concatenate(lw_rows, axis=0))

        uqk[0:8, :] = carry_s[...]
        y = convb_ref[...] + uqk[pl.ds(8, ts), :] * convw_ref[3:4, :]
        y = y + uqk[pl.ds(7, ts), :] * convw_ref[2:3, :]
        y = y + uqk[pl.ds(6, ts), :] * convw_ref[1:2, :]
        y = y + uqk[pl.ds(5, ts), :] * convw_ref[0:1, :]
        carry_s[...] = uqk[pl.ds(ts, 8), :]
        y = y * _sigmoid(y)
        for hh in range(MLSTM_HEADS):
            cols = slice(hh * hd, (hh + 1) * hd)
            qf = y[:, hh * hd:(hh + 1) * hd] * (hd ** -0.5)
            kf = y[:, mw + hh * hd:mw + (hh + 1) * hd]
            q_s[:, cols] = qf.astype(BF16)
            qw_s[:, cols] = (qf * wint[:, hh:hh + 1]).astype(BF16)
            k_s[:, cols] = kf.astype(BF16)
            kw_s[:, cols] = (kf * wkey[:, hh:hh + 1]).astype(BF16)

        ti = lax.broadcasted_iota(jnp.int32, (CHUNK, CHUNK), 0)
        si = lax.broadcasted_iota(jnp.int32, (CHUNK, CHUNK), 1)
        causal = ti >= si
        lane = lax.broadcasted_iota(jnp.int32, (CHUNK, gwk), 1)

        def level1(c):
            rows = pl.ds(c * CHUNK, CHUNK)
            dec = dec_s[c:c + 1, :]
            qk = []
            rhs = []
            for hh in range(MLSTM_HEADS):
                cols = slice(hh * hd, (hh + 1) * hd)
                v1h = v1_s[rows, 2 * hh * hd:(2 * hh + 2) * hd]
                c_prev = c_s[hh]
                qk.append(_dot_nt(q_s[rows, cols], k_s[rows, cols]))
                rhs.append(jnp.concatenate([c_prev.astype(BF16), v1h], axis=0))
                c_s[hh] = dec[:, hh:hh + 1] * c_prev + _dot_tn(kw_s[rows, cols], v1h)
            bg = bg_s[rows, :]
            bg_last = bg[CHUNK - 1:CHUNK, :]
            gq = gq_s[rows, :]
            gk = gk_s[rows, :]
            gv = gv_s[rows, :]
            qe = gq * jnp.exp(bg) * (gdk ** -0.5)
            ke = (gk * jnp.exp(-bg)).astype(BF16)
            kw = gk * jnp.exp(bg_last - bg)
            st_prev = st_s[...]
            qe_heads = []
            kw_heads = []
            for hh in range(GLA_HEADS):
                hm = (lane >= hh * gdk) & (lane < (hh + 1) * gdk)
                qe_heads.append(jnp.where(hm, qe, 0.0).astype(BF16))
                kw_heads.append(jnp.where(hm, kw, 0.0).astype(BF16))
            qe_stack = jnp.concatenate(qe_heads, axis=0)
            kw_stack = jnp.concatenate(kw_heads, axis=0)
            v_stack = jnp.concatenate(
                [gv[:, hh * gdv:(hh + 1) * gdv] for hh in range(GLA_HEADS)], axis=0)
            a_all = _dot_nt(qe_stack, ke)
            inter_all = _dot_nt(qe_stack, st_prev.astype(BF16))
            st_s[...] = st_prev * jnp.exp(bg_last) + _dot_tn(v_stack, kw_stack)
            return qk, rhs, a_all, inter_all

        gains = jnp.concatenate([mng_ref[...], gng_ref[...]], axis=1)

        def finish_head(rows, cols, hv):
            normed = hv * lax.rsqrt(jnp.mean(hv * hv, axis=-1, keepdims=True) + EPS)
            mix_s[rows, cols] = (normed * gains[:, cols] * gate_s[rows, cols]).astype(BF16)

        def level2(c, qk, rhs, a_all, inter_all):
            rows = pl.ds(c * CHUNK, CHUNK)
            mxc = mx_s[rows, :]
            em = emt_s[rows, :]
            gv = gv_s[rows, :]
            for hh in range(MLSTM_HEADS):
                cols = slice(hh * hd, (hh + 1) * hd)
                col = slice(hh, hh + 1)
                a_row = at_s[hh:hh + 1, c * CHUNK:(c + 1) * CHUNK]
                pm = jnp.exp(jnp.where(causal, a_row - mxc[:, col], -jnp.inf))
                lhs = jnp.concatenate([qw_s[rows, cols], (qk[hh] * pm).astype(BF16)], axis=1)
                r = _dot(lhs, rhs[hh])
                finish_head(rows, cols,
                            r[:, 0:hd] / jnp.maximum(jnp.abs(r[:, hd:2 * hd]), em[:, col]))
            for hh in range(GLA_HEADS):
                hrows = slice(hh * CHUNK, (hh + 1) * CHUNK)
                gcols = slice(hh * gdv, (hh + 1) * gdv)
                a_h = jnp.where(causal, a_all[hrows, :], 0.0).astype(BF16)
                intra = jnp.where(exact_intra, gintra_s[rows, gcols], _dot(a_h, gv[:, gcols]))
                finish_head(rows, slice(mw + hh * gdv, mw + (hh + 1) * gdv),
                            intra + inter_all[hrows, :])

        def next_a_piece():
            if a_pieces:
                a_pieces.pop(0)()

        keep_for_tail = 2
        pending = level1(0)
        for c in range(nc):
            current = pending
            if c + 1 < nc:
                pending = level1(c + 1)
                if len(a_pieces) > keep_for_tail:
                    next_a_piece()
            level2(c, *current)
            if c == nc // 4:
                normalise_next()

        while a_pieces:
            a_pieces.pop(0)()

        o_ref[0] = x_ref[0] + _dot(mix_s[...], wout_ref[...])

    j = pl.program_id(0)

    @pl.when(j == 0)
    def _():
        for raw in raw_sets:
            for hh in range(MLSTM_HEADS):
                raw[1][:, (2 * hh + 1) * hd:(2 * hh + 2) * hd] = jnp.ones((ts, hd), BF16)
        gintra_s[...] = jnp.zeros(gintra_s.shape, F32)
        flag_s[1] = jnp.int32(0)
        normalise(xb_ref, 0)
        for piece in stage_a_pieces(raw_sets[0], 0):
            piece()
        normalise(xa_ref, 1)

    @pl.when((j > 0) & ((j - 1) % nt == 0))
    def _():
        carry_s[...] = jnp.zeros(carry_s.shape, F32)
        c_s[...] = jnp.zeros(c_s.shape, F32)
        m_s[...] = jnp.zeros(m_s.shape, F32)
        st_s[...] = jnp.zeros(st_s.shape, F32)

    for parity in (0, 1):
        prev = 1 - parity

        @pl.when((j > 0) & (j % 2 == parity) & (flag_s[prev] != 0))
        def _():
            gla_intra_exact(raw_sets[prev])

        @pl.when((j > 0) & (j % 2 == parity))
        def _():
            stage_b(xb_ref, raw_sets[prev], flag_s[prev] != 0,
                    stage_a_pieces(raw_sets[parity], parity),
                    functools.partial(normalise, xa_ref, prev))


def _mixer(x, g, w_cat, sbias, conv_w, conv_b, wa2_pad, ba, mng, gng, w_out, *, ts, mw, gwk, gwv):
    B, S, D = x.shape
    ncol = w_cat.shape[1]
    nc = ts // CHUNK
    nt = S // ts
    n_tiles = B * nt
    hd = mw // MLSTM_HEADS
    gdv = gwv // GLA_HEADS
    assert gdv == hd and hd == LANES and gwk % LANES == 0 and S % ts == 0 and ts % (2 * CHUNK) == 0
    const = lambda j: (0, 0)

    def tile_a(j):
        t = jnp.minimum(j + 1, n_tiles - 1)
        return (t // nt, t % nt, 0)

    def tile_b(j):
        t = jnp.maximum(j - 1, 0)
        return (t // nt, t % nt, 0)

    single = pl.Buffered(1)
    col_f32 = pltpu.VMEM((ts, LANES), F32)
    raw_set = [
        pltpu.VMEM((ts + 8, 2 * mw), F32),
        pltpu.VMEM((ts, 2 * mw), BF16),
        pltpu.VMEM((ts, mw), F32),
        pltpu.VMEM((ts, gwk), F32),
        pltpu.VMEM((ts, gwk), F32),
        pltpu.VMEM((ts, gwv), BF16),
        pltpu.VMEM((ts, gwv), F32),
        col_f32,
        pltpu.VMEM((ts, gwk), F32),
    ]
    assert len(raw_set) == N_RAW
    kern = functools.partial(_mixer_kernel, ts=ts, nt=nt, mw=mw, gwk=gwk, gwv=gwv)
    return pl.pallas_call(
        kern,
        grid=(n_tiles + 1,),
        in_specs=[
            pl.BlockSpec((1, ts, D), tile_a),
            pl.BlockSpec((1, ts, D), tile_b),
            pl.BlockSpec((1, D), const),
            pl.BlockSpec((D, ncol), const, pipeline_mode=single),
            pl.BlockSpec((1, LANES), const),
            pl.BlockSpec((CONV_W, 2 * mw), const),
            pl.BlockSpec((1, 2 * mw), const),
            pl.BlockSpec((LANES, gwk), const),
            pl.BlockSpec((1, gwk), const),
            pl.BlockSpec((1, mw), const),
            pl.BlockSpec((1, gwv), const),
            pl.BlockSpec((mw + gwv, D), const, pipeline_mode=single),
        ],
        out_specs=pl.BlockSpec((1, ts, D), tile_b),
        out_shape=jax.ShapeDtypeStruct((B, S, D), F32),
        scratch_shapes=raw_set + raw_set + [
            pltpu.SMEM((2,), jnp.int32),
            pltpu.VMEM((ts, gwv), F32),
            pltpu.VMEM((2, ts, D), BF16),
            pltpu.VMEM((8, 2 * mw), F32),
            pltpu.VMEM((ts, mw), BF16),
            pltpu.VMEM((ts, mw), BF16),
            pltpu.VMEM((ts, mw), BF16),
            pltpu.VMEM((ts, mw), BF16),
            pltpu.VMEM((ts, mw + gwv), F32),
            pltpu.VMEM((ts, mw + gwv), BF16),
            col_f32,
            col_f32,
            pltpu.VMEM((LANES, ts), F32),
            col_f32,
            col_f32,
            pltpu.VMEM((max(nc, 8), LANES), F32),
            pltpu.VMEM((MLSTM_HEADS, hd, 2 * hd), F32),
            pltpu.VMEM((8, LANES), F32),
            pltpu.VMEM((gdv, gwk), F32),
        ],
        compiler_params=pltpu.CompilerParams(
            dimension_semantics=("arbitrary",), vmem_limit_bytes=VMEM_LIMIT_BYTES),
        name="mixer",
    )(x, x, g, w_cat, sbias, conv_w, conv_b, wa2_pad, ba, mng, gng, w_out)


def _attn_mlp_kernel(x_ref, k_ref, v_ref, xg_ref, wq_ref, wo_ref, mg_ref, w1_ref, w2_ref, fg_ref,
                     o_ref, *, ff_chunk):
    x = x_ref[...]
    D = x.shape[1]
    hd = D // XATTN_HEADS
    hq = _rmsnorm(x, xg_ref[...]).astype(BF16)
    q = (_dot(hq, wq_ref[...]) * (hd ** -0.5)).astype(BF16)
    atts = []
    for hh in range(XATTN_HEADS):
        cols = slice(hh * hd, (hh + 1) * hd)
        s = _dot_nt(q[:, cols], k_ref[0, :, cols])
        p = jnp.exp(s - jnp.max(s, axis=-1, keepdims=True))
        att = _dot(p.astype(BF16), v_ref[0, :, cols])
        atts.append((att / jnp.sum(p, axis=-1, keepdims=True)).astype(BF16))
    x = x + _dot(jnp.concatenate(atts, axis=1), wo_ref[...])

    hm = _rmsnorm(x, mg_ref[...]).astype(BF16)
    acc = x
    for c in range(w1_ref.shape[1] // ff_chunk):
        cols = slice(c * ff_chunk, (c + 1) * ff_chunk)
        t = jnp.maximum(_dot(hm, w1_ref[:, cols]), 0.0)
        acc = acc + _dot((t * t).astype(BF16), w2_ref[cols, :])
    o_ref[...] = _rmsnorm(acc, fg_ref[...])


def _attn_mlp(x2d, k, v, xg, wq, wo, mg, w1, w2, fg, *, tm, seq, ff_chunk):
    T, D = x2d.shape
    M = k.shape[1]
    F = w1.shape[1]
    assert T % tm == 0 and seq % tm == 0 and F % ff_chunk == 0
    per_b = seq // tm
    const = lambda i: (0, 0)
    single = pl.Buffered(1)
    kern = functools.partial(_attn_mlp_kernel, ff_chunk=ff_chunk)
    return pl.pallas_call(
        kern,
        grid=(T // tm,),
        in_specs=[
            pl.BlockSpec((tm, D), lambda i: (i, 0)),
            pl.BlockSpec((1, M, D), lambda i: (i // per_b, 0, 0)),
            pl.BlockSpec((1, M, D), lambda i: (i // per_b, 0, 0)),
            pl.BlockSpec((1, D), const),
            pl.BlockSpec((D, D), const, pipeline_mode=single),
            pl.BlockSpec((D, D), const, pipeline_mode=single),
            pl.BlockSpec((1, D), const),
            pl.BlockSpec((D, F), const, pipeline_mode=single),
            pl.BlockSpec((F, D), const, pipeline_mode=single),
            pl.BlockSpec((1, D), const),
        ],
        out_specs=pl.BlockSpec((tm, D), lambda i: (i, 0)),
        out_shape=jax.ShapeDtypeStruct((T, D), F32),
        compiler_params=pltpu.CompilerParams(
            dimension_semantics=("arbitrary",), vmem_limit_bytes=VMEM_LIMIT_BYTES),
        name="attn_mlp",
    )(x2d, k, v, xg, wq, wo, mg, w1, w2, fg)


MEM_KV_ROWS = 1024
MIXER_SEQ_TILE = 512
ATTN_MLP_TILE = 1024
FF_CHUNK = 1024


def kernel(x, mem, mix_norm_g, w_in, conv_w, conv_b, mlstm_i_b, mlstm_f_b, mlstm_norm_g, gla_wa2, gla_ba,
           gla_norm_g, w_out, xattn_norm_g, mem_norm_g, wq_x, wk_x, wv_x, wo_x, mlp_norm_g, w1, w2,
           final_norm_g):
    assert w_in.shape[0] == 1, "single-layer block"
    B, S, D = x.shape
    mw = mlstm_norm_g.shape[1]
    gwv = gla_norm_g.shape[1]
    rank, gwk = gla_wa2.shape[1:]
    nh = mlstm_i_b.shape[1]
    assert nh == MLSTM_HEADS and rank == GLA_RANK
    w_in = w_in[0].astype(BF16)

    o_mi = 4 * mw
    o_gq = o_mi + 2 * nh
    o_ga = o_gq + 2 * gwk + 2 * gwv
    small_w = jnp.concatenate(
        [w_in[:, o_mi:o_gq], w_in[:, o_ga:o_ga + rank],
         jnp.zeros((D, LANES - 2 * nh - rank), w_in.dtype)], axis=1)
    w_cat = jnp.concatenate([w_in[:, 0:o_mi], w_in[:, o_gq:o_ga], small_w], axis=1)
    sbias = jnp.concatenate([mlstm_i_b[0], mlstm_f_b[0], jnp.zeros((LANES - 2 * nh,), F32)])[None, :]
    wa2_pad = jnp.zeros((LANES, gwk), F32).at[2 * nh:2 * nh + rank].set(gla_wa2[0]).astype(BF16)

    k_mem, v_mem = _mem_kv(mem, mem_norm_g, wk_x[0].astype(BF16), wv_x[0].astype(BF16),
                           rows=MEM_KV_ROWS)
    x = _mixer(x, mix_norm_g, w_cat, sbias, conv_w[0], conv_b, wa2_pad, gla_ba,
               mlstm_norm_g, gla_norm_g, w_out[0].astype(BF16),
               ts=MIXER_SEQ_TILE, mw=mw, gwk=gwk, gwv=gwv)
    out = _attn_mlp(x.reshape(B * S, D), k_mem, v_mem, xattn_norm_g, wq_x[0].astype(BF16),
                    wo_x[0].astype(BF16), mlp_norm_g, w1[0].astype(BF16), w2[0].astype(BF16),
                    final_norm_g[None, :], tm=ATTN_MLP_TILE, seq=S, ff_chunk=FF_CHUNK)
    return out.reshape(B, S, D)
```

```python
import functools

import jax
import jax.numpy as jnp
from jax import lax
from jax.experimental import pallas as pl
from jax.experimental.pallas import tpu as pltpu

EPS = 1e-6
CHUNK = 64
CONV_W = 4
MLSTM_HEADS = 4
GLA_HEADS = 4
GLA_RANK = 16
GLA_TAU = 16.0
XATTN_HEADS = 4
LANES = 128
MXU_COLS = 256
VMEM_LIMIT_BYTES = 56 * 1024 * 1024

BF16 = jnp.bfloat16
F32 = jnp.float32


def _rmsnorm(x, g):
    return x * lax.rsqrt(jnp.mean(x * x, axis=-1, keepdims=True) + EPS) * g


def _log_sigmoid(z):
    return jnp.minimum(z, 0.0) - jnp.log(1.0 + jnp.exp(-jnp.abs(z)))


def _sigmoid(z):
    return 0.5 * jnp.tanh(0.5 * z) + 0.5


def _dot(a, b):
    return jnp.dot(a, b, preferred_element_type=F32)


def _dot_nt(a, b):
    return lax.dot_general(a, b, (((1,), (1,)), ((), ())), preferred_element_type=F32)


def _dot_tn(a, b):
    return lax.dot_general(a, b, (((0,), (0,)), ((), ())), preferred_element_type=F32)


def _segment_scan(x, seg, op, identity):
    row = lax.broadcasted_iota(jnp.int32, x.shape, 0) & (seg - 1)
    k = 1
    while k < seg:
        shifted = pltpu.roll(x, k, axis=0)
        x = op(x, jnp.where(row >= k, shifted, identity))
        k *= 2
    return x


def _mem_kv_kernel(mem_ref, g_ref, wk_ref, wv_ref, k_ref, v_ref):
    mn = _rmsnorm(mem_ref[...], g_ref[...]).astype(BF16)
    k_ref[...] = _dot(mn, wk_ref[...]).astype(BF16)
    v_ref[...] = _dot(mn, wv_ref[...]).astype(BF16)


def _mem_kv(mem, g, wk, wv, *, rows):
    B, M, D = mem.shape
    assert (B * M) % rows == 0
    const = lambda i: (0, 0)
    tile = lambda i: (i, 0)
    k, v = pl.pallas_call(
        _mem_kv_kernel,
        grid=(B * M // rows,),
        in_specs=[
            pl.BlockSpec((rows, D), tile),
            pl.BlockSpec((1, D), const),
            pl.BlockSpec((D, D), const),
            pl.BlockSpec((D, D), const),
        ],
        out_specs=[pl.BlockSpec((rows, D), tile), pl.BlockSpec((rows, D), tile)],
        out_shape=[jax.ShapeDtypeStruct((B * M, D), BF16)] * 2,
        compiler_params=pltpu.CompilerParams(
            dimension_semantics=("arbitrary",), vmem_limit_bytes=VMEM_LIMIT_BYTES),
        name="mem_kv",
    )(mem.reshape(B * M, D), g, wk, wv)
    return k.reshape(B, M, D), v.reshape(B, M, D)


N_RAW = 9
GLA_SAFE_LOG_RANGE = 60.0


def _mixer_kernel(xa_ref, xb_ref, g_ref, w_ref, sbias_ref, convw_ref, convb_ref, wa2_ref, ba_ref,
                  mng_ref, gng_ref, wout_ref,
                  o_ref,
                  *scratch, ts, nt, mw, gwk, gwv):
    raw_sets = (scratch[0:N_RAW], scratch[N_RAW:2 * N_RAW])
    (flag_s, gintra_s, h_s, carry_s, q_s, qw_s, k_s, kw_s, gate_s, mix_s, b_s, cm_s, at_s, mx_s,
     emt_s, dec_s, c_s, m_s, st_s) = scratch[2 * N_RAW:]
    nc = ts // CHUNK
    hd = mw // MLSTM_HEADS
    gdk = gwk // GLA_HEADS
    gdv = gwv // GLA_HEADS
    o_v = 2 * mw
    o_o = 3 * mw
    o_gq = 4 * mw
    o_gk = o_gq + gwk
    o_gv = o_gk + gwk
    o_gr = o_gv + gwv
    o_sm = o_gr + gwv

    def normalise(x_ref, slot):
        h_s[slot] = _rmsnorm(x_ref[0], g_ref[...]).astype(BF16)

    def stage_a_pieces(raw, slot):
        uqk, v1, uo, ugq, ugk, ugv, ugr, usm, ubg = raw

        def proj(lo, hi):
            return _dot(h_s[slot], w_ref[:, lo:hi])

        def piece(lo, width, store):
            return lambda: store(proj(lo, lo + width))

        def store_to(ref, row0, c0, dtype):
            def store(u):
                ref[pl.ds(row0, ts), c0:c0 + u.shape[1]] = u.astype(dtype)
            return store

        def store_v(first_head):
            def store(u):
                for i in range(u.shape[1] // hd):
                    hh = first_head + i
                    v1[:, 2 * hh * hd:(2 * hh + 1) * hd] = u[:, i * hd:(i + 1) * hd].astype(BF16)
            return store

        w = MXU_COLS
        pieces = [piece(o_sm, LANES, store_to(usm, 0, 0, F32))]
        for c0 in range(0, 2 * mw, w):
            pieces.append(piece(c0, w, store_to(uqk, 8, c0, F32)))
        for c0 in range(0, mw, w):
            pieces.append(piece(o_v + c0, w, store_v(c0 // hd)))
        for c0 in range(0, mw, w):
            pieces.append(piece(o_o + c0, w, store_to(uo, 0, c0, F32)))
        for c0 in range(0, gwk, w):
            pieces.append(piece(o_gq + c0, w, store_to(ugq, 0, c0, F32)))
        for c0 in range(0, gwk, w):
            pieces.append(piece(o_gk + c0, w, store_to(ugk, 0, c0, F32)))
        for c0 in range(0, gwv, w):
            pieces.append(piece(o_gv + c0, w, store_to(ugv, 0, c0, BF16)))
        for c0 in range(0, gwv, w):
            pieces.append(piece(o_gr + c0, w, store_to(ugr, 0, c0, F32)))

        def p_bg():
            small = usm[...] + sbias_ref[...]
            la = _log_sigmoid(_dot(small.astype(BF16), wa2_ref[...]) + ba_ref[...]) * (1.0 / GLA_TAU)
            bg = _segment_scan(la, CHUNK, jnp.add, 0.0)
            ubg[...] = bg
            flag_s[slot] = (jnp.min(bg) < -GLA_SAFE_LOG_RANGE).astype(jnp.int32)

        half = (len(pieces) + 1) // 2
        return pieces[:half] + [p_bg] + pieces[half:]

    def gla_intra_exact(raw):
        ugq, ugk, ugv, ubg = raw[3], raw[4], raw[5], raw[8]
        s_idx = lax.broadcasted_iota(jnp.int32, (CHUNK, gwk), 0)
        t_idx = lax.broadcasted_iota(jnp.int32, (CHUNK, gwv), 0)
        ind =(lax.broadcasted_iota(jnp.int32, (gwk, LANES), 0) // gdk
               == lax.broadcasted_iota(jnp.int32, (gwk, LANES), 1)).astype(BF16)

        def chunk_body(c, carry):
            r0 = pl.multiple_of(c * CHUNK, CHUNK)
            bgc = ubg[pl.ds(r0, CHUNK), :]
            gkc = ugk[pl.ds(r0, CHUNK), :]
            gvc = ugv[pl.ds(r0, CHUNK), :]

            def row_body(t, acc):
                bgt = ubg[pl.ds(r0 + t, 1), :]
                gqt = ugq[pl.ds(r0 + t, 1), :] * (gdk ** -0.5)
                valid = s_idx <= t
                e = jnp.exp(jnp.where(valid, bgt - bgc, 0.0))
                g = jnp.where(valid, gqt * gkc * e, 0.0).astype(BF16)
                a_cols = _dot(g, ind).astype(BF16)
                res = _dot_tn(a_cols, gvc)
                row = jnp.concatenate(
                    [res[hh:hh + 1, hh * gdv:(hh + 1) * gdv] for hh in range(GLA_HEADS)], axis=1)
                return jnp.where(t_idx == t, row, acc)

            gintra_s[pl.ds(r0, CHUNK), :] = lax.fori_loop(
                0, CHUNK, row_body, jnp.zeros((CHUNK, gwv), F32))
            return carry

        lax.fori_loop(0, nc, chunk_body, 0)

    def stage_b(x_ref, raw, exact_intra, a_pieces, normalise_next):
        a_pieces = list(a_pieces)
        for _ in range(len(a_pieces) // 2):
            a_pieces.pop(0)()
        uqk, v1_s, uo, gq_s, gk_s, gv_s, ugr, usm, bg_s = raw
        gate_s[:, 0:mw] = _sigmoid(uo[...])
        gr = ugr[...]
        gate_s[:, mw:mw + gwv] = gr * _sigmoid(gr)
        small = usm[...] + sbias_ref[...]

        b = pltpu.roll(_segment_scan(_log_sigmoid(small), CHUNK, jnp.add, 0.0),
                       LANES - MLSTM_HEADS, axis=1)
        a = small - b
        cm = _segment_scan(a, CHUNK, jnp.maximum, -jnp.inf)
        b_s[...] = b
        cm_s[...] = cm
        b_last = b_s[pl.ds(CHUNK - 1, nc, stride=CHUNK), :]
        cm_last = cm_s[pl.ds(CHUNK - 1, nc, stride=CHUNK), :]
        m = m_s[0:1, :]
        m_rows = []
        lw_rows = []
        for c in range(nc):
            bl = b_last[c:c + 1, :]
            gmax = bl + cm_last[c:c + 1, :]
            m_new = jnp.maximum(bl + m, gmax)
            dec_s[c:c + 1, :] = jnp.exp(bl + m - m_new)
            m_rows.append(jnp.broadcast_to(m, (CHUNK, LANES)))
            lw_rows.append(jnp.broadcast_to(gmax - m_new - cm_last[c:c + 1, :], (CHUNK, LANES)))
            m = m_new
        m_s[0:1, :] = m
        mc = jnp.concatenate(m_rows, axis=0)
        mx = jnp.maximum(cm, mc)
        mx_s[...] = mx
        emt_s[...] = jnp.exp(-(b + mx))
        at_s[...] = a.T
        wint = jnp.exp(mc - mx)
        wkey = jnp.exp(a + jnp.concatenate(lw_rows, axis=0))

        uqk[0:8, :] = carry_s[...]
        y = convb_ref[...] + uqk[pl.ds(8, ts), :] * convw_ref[3:4, :]
        y = y + uqk[pl.ds(7, ts), :] * convw_ref[2:3, :]
        y = y + uqk[pl.ds(6, ts), :] * convw_ref[1:2, :]
        y = y + uqk[pl.ds(5, ts), :] * convw_ref[0:1, :]
        carry_s[...] = uqk[pl.ds(ts, 8), :]
        y = y * _sigmoid(y)
        for hh in range(MLSTM_HEADS):
            cols = slice(hh * hd, (hh + 1) * hd)
            qf = y[:, hh * hd:(hh + 1) * hd] * (hd ** -0.5)
            kf = y[:, mw + hh * hd:mw + (hh + 1) * hd]
            q_s[:, cols] = qf.astype(BF16)
            qw_s[:, cols] = (qf * wint[:, hh:hh + 1]).astype(BF16)
            k_s[:, cols] = kf.astype(BF16)
            kw_s[:, cols] = (kf * wkey[:, hh:hh + 1]).astype(BF16)

        ti = lax.broadcasted_iota(jnp.int32, (CHUNK, CHUNK), 0)
        si = lax.broadcasted_iota(jnp.int32, (CHUNK, CHUNK), 1)
        causal = ti >= si
        lane = lax.broadcasted_iota(jnp.int32, (CHUNK, gwk), 1)

        def level1(c):
            rows = pl.ds(c * CHUNK, CHUNK)
            dec = dec_s[c:c + 1, :]
            qk = []
            rhs = []
            for hh in range(MLSTM_HEADS):
                cols = slice(hh * hd, (hh + 1) * hd)
                v1h = v1_s[rows, 2 * hh * hd:(2 * hh + 2) * hd]
                c_prev = c_s[hh]
                qk.append(_dot_nt(q_s[rows, cols], k_s[rows, cols]))
                rhs.append(jnp.concatenate([c_prev.astype(BF16), v1h], axis=0))
                c_s[hh] = dec[:, hh:hh + 1] * c_prev + _dot_tn(kw_s[rows, cols], v1h)
            bg = bg_s[rows, :]
            bg_last = bg[CHUNK - 1:CHUNK, :]
            gq = gq_s[rows, :]
            gk = gk_s[rows, :]
            gv = gv_s[rows, :]
            qe = gq * jnp.exp(bg) * (gdk ** -0.5)
            ke = (gk * jnp.exp(-bg)).astype(BF16)
            kw = gk * jnp.exp(bg_last - bg)
            st_prev = st_s[...]
            qe_heads = []
            kw_heads = []
            for hh in range(GLA_HEADS):
                hm = (lane >= hh * gdk) & (lane < (hh + 1) * gdk)
                qe_heads.append(jnp.where(hm, qe, 0.0).astype(BF16))
                kw_heads.append(jnp.where(hm, kw, 0.0).astype(BF16))
            qe_stack = jnp.concatenate(qe_heads, axis=0)
            kw_stack = jnp.concatenate(kw_heads, axis=0)
            v_stack = jnp.concatenate(
                [gv[:, hh * gdv:(hh + 1) * gdv] for hh in range(GLA_HEADS)], axis=0)
            a_all = _dot_nt(qe_stack, ke)
            inter_all = _dot_nt(qe_stack, st_prev.astype(BF16))
            st_s[...] = st_prev * jnp.exp(bg_last) + _dot_tn(v_stack, kw_stack)
            return qk, rhs, a_all, inter_all

        gains = jnp.concatenate([mng_ref[...], gng_ref[...]], axis=1)

        def finish_head(rows, cols, hv):
            normed = hv * lax.rsqrt(jnp.mean(hv * hv, axis=-1, keepdims=True) + EPS)
            mix_s[rows, cols] = (normed * gains[:, cols] * gate_s[rows, cols]).astype(BF16)

        def level2(c, qk, rhs, a_all, inter_all):
            rows = pl.ds(c * CHUNK, CHUNK)
            mxc = mx_s[rows, :]
            em = emt_s[rows, :]
            gv = gv_s[rows, :]
            for hh in range(MLSTM_HEADS):
                cols = slice(hh * hd, (hh + 1) * hd)
                col = slice(hh, hh + 1)
                a_row = at_s[hh:hh + 1, c * CHUNK:(c + 1) * CHUNK]
                pm = jnp.exp(jnp.where(causal, a_row - mxc[:, col], -jnp.inf))
                lhs = jnp.concatenate([qw_s[rows, cols], (qk[hh] * pm).astype(BF16)], axis=1)
                r = _dot(lhs, rhs[hh])
                finish_head(rows, cols,
                            r[:, 0:hd] / jnp.maximum(jnp.abs(r[:, hd:2 * hd]), em[:, col]))
            for hh in range(GLA_HEADS):
                hrows = slice(hh * CHUNK, (hh + 1) * CHUNK)
                gcols = slice(hh * gdv, (hh + 1) * gdv)
                a_h = jnp.where(causal, a_all[hrows, :], 0.0).astype(BF16)
                intra = jnp.where(exact_intra, gintra_s[rows, gcols], _dot(a_h, gv[:, gcols]))
                finish_head(rows, slice(mw + hh * gdv, mw + (hh + 1) * gdv),
                            intra + inter_all[hrows, :])

        def next_a_piece():
            if a_pieces:
                a_pieces.pop(0)()

        keep_for_tail = 2
        pending = level1(0)
        for c in range(nc):
            current = pending
            if c + 1 < nc:
                pending = level1(c + 1)
                if len(a_pieces) > keep_for_tail:
                    next_a_piece()
            level2(c, *current)
            if c == nc // 4:
                normalise_next()

        while a_pieces:
            a_pieces.pop(0)()

        o_ref[0] = x_ref[0] + _dot(mix_s[...], wout_ref[...])

    j = pl.program_id(0)

    @pl.when(j == 0)
    def _():
        for raw in raw_sets:
            for hh in range(MLSTM_HEADS):
                raw[1][:, (2 * hh + 1) * hd:(2 * hh + 2) * hd] = jnp.ones((ts, hd), BF16)
        gintra_s[...] = jnp.zeros(gintra_s.shape, F32)
        flag_s[1] = jnp.int32(0)
        normalise(xb_ref, 0)
        for piece in stage_a_pieces(raw_sets[0], 0):
            piece()
        normalise(xa_ref, 1)

    @pl.when((j > 0) & ((j - 1) % nt == 0))
    def _():
        carry_s[...] = jnp.zeros(carry_s.shape, F32)
        c_s[...] = jnp.zeros(c_s.shape, F32)
        m_s[...] = jnp.zeros(m_s.shape, F32)
        st_s[...] = jnp.zeros(st_s.shape, F32)

    for parity in (0, 1):
        prev = 1 - parity

        @pl.when((j > 0) & (j % 2 == parity) & (flag_s[prev] != 0))
        def _():
            gla_intra_exact(raw_sets[prev])

        @pl.when((j > 0) & (j % 2 == parity))
        def _():
            stage_b(xb_ref, raw_sets[prev], flag_s[prev] != 0,
                    stage_a_pieces(raw_sets[parity], parity),
                    functools.partial(normalise, xa_ref, prev))


def _mixer(x, g, w_cat, sbias, conv_w, conv_b, wa2_pad, ba, mng, gng, w_out, *, ts, mw, gwk, gwv):
    B, S, D = x.shape
    ncol = w_cat.shape[1]
    nc = ts // CHUNK
    nt = S // ts
    n_tiles = B * nt
    hd = mw // MLSTM_HEADS
    gdv = gwv // GLA_HEADS
    assert gdv == hd and hd == LANES and gwk % LANES == 0 and S % ts == 0 and ts % (2 * CHUNK) == 0
    const = lambda j: (0, 0)

    def tile_a(j):
        t = jnp.minimum(j + 1, n_tiles - 1)
        return (t // nt, t % nt, 0)

    def tile_b(j):
        t = jnp.maximum(j - 1, 0)
        return (t // nt, t % nt, 0)

    single = pl.Buffered(1)
    col_f32 = pltpu.VMEM((ts, LANES), F32)
    raw_set = [
        pltpu.VMEM((ts + 8, 2 * mw), F32),
        pltpu.VMEM((ts, 2 * mw), BF16),
        pltpu.VMEM((ts, mw), F32),
        pltpu.VMEM((ts, gwk), F32),
        pltpu.VMEM((ts, gwk), F32),
        pltpu.VMEM((ts, gwv), BF16),
        pltpu.VMEM((ts, gwv), F32),
        col_f32,
        pltpu.VMEM((ts, gwk), F32),
    ]
    assert len(raw_set) == N_RAW
    kern = functools.partial(_mixer_kernel, ts=ts, nt=nt, mw=mw, gwk=gwk, gwv=gwv)
    return pl.pallas_call(
        kern,
        grid=(n_tiles + 1,),
        in_specs=[
            pl.BlockSpec((1, ts, D), tile_a),
            pl.BlockSpec((1, ts, D), tile_b),
            pl.BlockSpec((1, D), const),
            pl.BlockSpec((D, ncol), const, pipeline_mode=single),
            pl.BlockSpec((1, LANES), const),
            pl.BlockSpec((CONV_W, 2 * mw), const),
            pl.BlockSpec((1, 2 * mw), const),
            pl.BlockSpec((LANES, gwk), const),
            pl.BlockSpec((1, gwk), const),
            pl.BlockSpec((1, mw), const),
            pl.BlockSpec((1, gwv), const),
            pl.BlockSpec((mw + gwv, D), const, pipeline_mode=single),
        ],
        out_specs=pl.BlockSpec((1, ts, D), tile_b),
        out_shape=jax.ShapeDtypeStruct((B, S, D), F32),
        scratch_shapes=raw_set + raw_set + [
            pltpu.SMEM((2,), jnp.int32),
            pltpu.VMEM((ts, gwv), F32),
            pltpu.VMEM((2, ts, D), BF16),
            pltpu.VMEM((8, 2 * mw), F32),
            pltpu.VMEM((ts, mw), BF16),
            pltpu.VMEM((ts, mw), BF16),
            pltpu.VMEM((ts, mw), BF16),
            pltpu.VMEM((ts, mw), BF16),
            pltpu.VMEM((ts, mw + gwv), F32),
            pltpu.VMEM((ts, mw + gwv), BF16),
            col_f32,
            col_f32,
            pltpu.VMEM((LANES, ts), F32),
            col_f32,
            col_f32,
            pltpu.VMEM((max(nc, 8), LANES), F32),
            pltpu.VMEM((MLSTM_HEADS, hd, 2 * hd), F32),
            pltpu.VMEM((8, LANES), F32),
            pltpu.VMEM((gdv, gwk), F32),
        ],
        compiler_params=pltpu.CompilerParams(
            dimension_semantics=("arbitrary",), vmem_limit_bytes=VMEM_LIMIT_BYTES),
        name="mixer",
    )(x, x, g, w_cat, sbias, conv_w, conv_b, wa2_pad, ba, mng, gng, w_out)


def _attn_mlp_kernel(x_ref, k_ref, v_ref, xg_ref, wq_ref, wo_ref, mg_ref, w1_ref, w2_ref, fg_ref,
                     o_ref, *, ff_chunk):
    x = x_ref[...]
    D = x.shape[1]
    hd = D // XATTN_HEADS
    hq = _rmsnorm(x, xg_ref[...]).astype(BF16)
    q = (_dot(hq, wq_ref[...]) * (hd ** -0.5)).astype(BF16)
    atts = []
    for hh in range(XATTN_HEADS):
        cols = slice(hh * hd, (hh + 1) * hd)
        s = _dot_nt(q[:, cols], k_ref[0, :, cols])
        p = jnp.exp(s - jnp.max(s, axis=-1, keepdims=True))
        att = _dot(p.astype(BF16), v_ref[0, :, cols])
        atts.append((att / jnp.sum(p, axis=-1, keepdims=True)).astype(BF16))
    x = x + _dot(jnp.concatenate(atts, axis=1), wo_ref[...])

    hm = _rmsnorm(x, mg_ref[...]).astype(BF16)
    acc = x
    for c in range(w1_ref.shape[1] // ff_chunk):
        cols = slice(c * ff_chunk, (c + 1) * ff_chunk)
        t = jnp.maximum(_dot(hm, w1_ref[:, cols]), 0.0)
        acc = acc + _dot((t * t).astype(BF16), w2_ref[cols, :])
    o_ref[...] = _rmsnorm(acc, fg_ref[...])


def _attn_mlp(x2d, k, v, xg, wq, wo, mg, w1, w2, fg, *, tm, seq, ff_chunk):
    T, D = x2d.shape
    M = k.shape[1]
    F = w1.shape[1]
    assert T % tm == 0 and seq % tm == 0 and F % ff_chunk == 0
    per_b = seq // tm
    const = lambda i: (0, 0)
    single = pl.Buffered(1)
    kern = functools.partial(_attn_mlp_kernel, ff_chunk=ff_chunk)
    return pl.pallas_call(
        kern,
        grid=(T // tm,),
        in_specs=[
            pl.BlockSpec((tm, D), lambda i: (i, 0)),
            pl.BlockSpec((1, M, D), lambda i: (i // per_b, 0, 0)),
            pl.BlockSpec((1, M, D), lambda i: (i // per_b, 0, 0)),
            pl.BlockSpec((1, D), const),
            pl.BlockSpec((D, D), const, pipeline_mode=single),
            pl.BlockSpec((D, D), const, pipeline_mode=single),
            pl.BlockSpec((1, D), const),
            pl.BlockSpec((D, F), const, pipeline_mode=single),
            pl.BlockSpec((F, D), const, pipeline_mode=single),
            pl.BlockSpec((1, D), const),
        ],
        out_specs=pl.BlockSpec((tm, D), lambda i: (i, 0)),
        out_shape=jax.ShapeDtypeStruct((T, D), F32),
        compiler_params=pltpu.CompilerParams(
            dimension_semantics=("arbitrary",), vmem_limit_bytes=VMEM_LIMIT_BYTES),
        name="attn_mlp",
    )(x2d, k, v, xg, wq, wo, mg, w1, w2, fg)


MEM_KV_ROWS = 1024
MIXER_SEQ_TILE = 512
ATTN_MLP_TILE = 1024
FF_CHUNK = 1024


def kernel(x, mem, mix_norm_g, w_in, conv_w, conv_b, mlstm_i_b, mlstm_f_b, mlstm_norm_g, gla_wa2, gla_ba,
           gla_norm_g, w_out, xattn_norm_g, mem_norm_g, wq_x, wk_x, wv_x, wo_x, mlp_norm_g, w1, w2,
           final_norm_g):
    assert w_in.shape[0] == 1, "single-layer block"
    B, S, D = x.shape
    mw = mlstm_norm_g.shape[1]
    gwv = gla_norm_g.shape[1]
    rank, gwk = gla_wa2.shape[1:]
    nh = mlstm_i_b.shape[1]
    assert nh == MLSTM_HEADS and rank == GLA_RANK
    w_in = lax.optimization_barrier(w_in[0].astype(BF16))

    o_mi = 4 * mw
    o_gq = o_mi + 2 * nh
    o_ga = o_gq + 2 * gwk + 2 * gwv
    small_w = jnp.concatenate(
        [w_in[:, o_mi:o_gq], w_in[:, o_ga:o_ga + rank],
         jnp.zeros((D, LANES - 2 * nh - rank), w_in.dtype)], axis=1)
    w_cat = jnp.concatenate([w_in[:, 0:o_mi], w_in[:, o_gq:o_ga], small_w], axis=1)
    sbias = jnp.concatenate([mlstm_i_b[0], mlstm_f_b[0], jnp.zeros((LANES - 2 * nh,), F32)])[None, :]
    wa2_pad = jnp.zeros((LANES, gwk), F32).at[2 * nh:2 * nh + rank].set(gla_wa2[0]).astype(BF16)

    k_mem, v_mem = _mem_kv(mem, mem_norm_g, wk_x[0].astype(BF16), wv_x[0].astype(BF16),
                           rows=MEM_KV_ROWS)
    x = _mixer(x, mix_norm_g, w_cat, sbias, conv_w[0], conv_b, wa2_pad, gla_ba,
               mlstm_norm_g, gla_norm_g, w_out[0].astype(BF16),
               ts=MIXER_SEQ_TILE, mw=mw, gwk=gwk, gwv=gwv)
    out = _attn_mlp(x.reshape(B * S, D), k_mem, v_mem, xattn_norm_g, wq_x[0].astype(BF16),
                    wo_x[0].astype(BF16), mlp_norm_g, w1[0].astype(BF16), w2[0].astype(BF16),
                    final_norm_g[None, :], tm=ATTN_MLP_TILE, seq=S, ff_chunk=FF_CHUNK)
    return out.reshape(B, S, D)
```

```python
import functools

import jax
import jax.numpy as jnp
from jax import lax
from jax.experimental import pallas as pl
from jax.experimental.pallas import tpu as pltpu

EPS = 1e-6
CHUNK = 128
CONV_W = 4
MLSTM_HEADS = 4
GLA_HEADS = 4
GLA_RANK = 16
GLA_TAU = 16.0
XATTN_HEADS = 4
LANES = 128
MXU_COLS = 256
VMEM_LIMIT_BYTES = 56 * 1024 * 1024

BF16 = jnp.bfloat16
F32 = jnp.float32


def _rmsnorm(x, g):
    return x * lax.rsqrt(jnp.mean(x * x, axis=-1, keepdims=True) + EPS) * g


def _log_sigmoid(z):
    return jnp.minimum(z, 0.0) - jnp.log(1.0 + jnp.exp(-jnp.abs(z)))


def _sigmoid(z):
    return 0.5 * jnp.tanh(0.5 * z) + 0.5


def _dot(a, b):
    return jnp.dot(a, b, preferred_element_type=F32)


def _dot_nt(a, b):
    return lax.dot_general(a, b, (((1,), (1,)), ((), ())), preferred_element_type=F32)


def _dot_tn(a, b):
    return lax.dot_general(a, b, (((0,), (0,)), ((), ())), preferred_element_type=F32)


def _segment_scan(x, seg, op, identity):
    row = lax.broadcasted_iota(jnp.int32, x.shape, 0) & (seg - 1)
    k = 1
    while k < seg:
        shifted = pltpu.roll(x, k, axis=0)
        x = op(x, jnp.where(row >= k, shifted, identity))
        k *= 2
    return x


def _mem_kv_kernel(mem_ref, g_ref, wk_ref, wv_ref, k_ref, v_ref):
    mn = _rmsnorm(mem_ref[...], g_ref[...]).astype(BF16)
    k_ref[...] = _dot(mn, wk_ref[...]).astype(BF16)
    v_ref[...] = _dot(mn, wv_ref[...]).astype(BF16)


def _mem_kv(mem, g, wk, wv, *, rows):
    B, M, D = mem.shape
    assert (B * M) % rows == 0
    const = lambda i: (0, 0)
    tile = lambda i: (i, 0)
    k, v = pl.pallas_call(
        _mem_kv_kernel,
        grid=(B * M // rows,),
        in_specs=[
            pl.BlockSpec((rows, D), tile),
            pl.BlockSpec((1, D), const),
            pl.BlockSpec((D, D), const),
            pl.BlockSpec((D, D), const),
        ],
        out_specs=[pl.BlockSpec((rows, D), tile), pl.BlockSpec((rows, D), tile)],
        out_shape=[jax.ShapeDtypeStruct((B * M, D), BF16)] * 2,
        compiler_params=pltpu.CompilerParams(
            dimension_semantics=("arbitrary",), vmem_limit_bytes=VMEM_LIMIT_BYTES),
        name="mem_kv",
    )(mem.reshape(B * M, D), g, wk, wv)
    return k.reshape(B, M, D), v.reshape(B, M, D)


N_RAW = 9
GLA_SAFE_LOG_RANGE = 60.0


def _mixer_kernel(xa_ref, xb_ref, g_ref, w_ref, sbias_ref, convw_ref, convb_ref, wa2_ref, ba_ref,
                  mng_ref, gng_ref, wout_ref,
                  o_ref,
                  *scratch, ts, nt, mw, gwk, gwv):
    raw_sets = (scratch[0:N_RAW], scratch[N_RAW:2 * N_RAW])
    (flag_s, gintra_s, h_s, carry_s, q_s, qw_s, k_s, kw_s, gate_s, mix_s, b_s, cm_s, at_s, mx_s,
     emt_s, dec_s, c_s, m_s, st_s) = scratch[2 * N_RAW:]
    nc = ts // CHUNK
    hd = mw // MLSTM_HEADS
    gdk = gwk // GLA_HEADS
    gdv = gwv // GLA_HEADS
    o_v = 2 * mw
    o_o = 3 * mw
    o_gq = 4 * mw
    o_gk = o_gq + gwk
    o_gv = o_gk + gwk
    o_gr = o_gv + gwv
    o_sm = o_gr + gwv

    def normalise(x_ref, slot):
        h_s[slot] = _rmsnorm(x_ref[0], g_ref[...]).astype(BF16)

    def stage_a_pieces(raw, slot):
        uqk, v1, uo, ugq, ugk, ugv, ugr, usm, ubg = raw

        def proj(lo, hi):
            return _dot(h_s[slot], w_ref[:, lo:hi])

        def piece(lo, width, store):
            return lambda: store(proj(lo, lo + width))

        def store_to(ref, row0, c0, dtype):
            def store(u):
                ref[pl.ds(row0, ts), c0:c0 + u.shape[1]] = u.astype(dtype)
            return store

        def store_v(first_head):
            def store(u):
                for i in range(u.shape[1] // hd):
                    hh = first_head + i
                    v1[:, 2 * hh * hd:(2 * hh + 1) * hd] = u[:, i * hd:(i + 1) * hd].astype(BF16)
            return store

        w = MXU_COLS
        pieces = [piece(o_sm, LANES, store_to(usm, 0, 0, F32))]
        for c0 in range(0, 2 * mw, w):
            pieces.append(piece(c0, w, store_to(uqk, 8, c0, F32)))
        for c0 in range(0, mw, w):
            pieces.append(piece(o_v + c0, w, store_v(c0 // hd)))
        for c0 in range(0, mw, w):
            pieces.append(piece(o_o + c0, w, store_to(uo, 0, c0, F32)))
        for c0 in range(0, gwk, w):
            pieces.append(piece(o_gq + c0, w, store_to(ugq, 0, c0, F32)))
        for c0 in range(0, gwk, w):
            pieces.append(piece(o_gk + c0, w, store_to(ugk, 0, c0, F32)))
        for c0 in range(0, gwv, w):
            pieces.append(piece(o_gv + c0, w, store_to(ugv, 0, c0, BF16)))
        for c0 in range(0, gwv, w):
            pieces.append(piece(o_gr + c0, w, store_to(ugr, 0, c0, F32)))

        def p_bg():
            small = usm[...] + sbias_ref[...]
            la = _log_sigmoid(_dot(small.astype(BF16), wa2_ref[...]) + ba_ref[...]) * (1.0 / GLA_TAU)
            bg = _segment_scan(la, CHUNK, jnp.add, 0.0)
            ubg[...] = bg
            flag_s[slot] = (jnp.min(bg) < -GLA_SAFE_LOG_RANGE).astype(jnp.int32)

        half = (len(pieces) + 1) // 2
        return pieces[:half] + [p_bg] + pieces[half:]

    def gla_intra_exact(raw):
        ugq, ugk, ugv, ubg = raw[3], raw[4], raw[5], raw[8]
        s_idx = lax.broadcasted_iota(jnp.int32, (CHUNK, gwk), 0)
        t_idx = lax.broadcasted_iota(jnp.int32, (CHUNK, gwv), 0)
        ind =(lax.broadcasted_iota(jnp.int32, (gwk, LANES), 0) // gdk
               == lax.broadcasted_iota(jnp.int32, (gwk, LANES), 1)).astype(BF16)

        def chunk_body(c, carry):
            r0 = pl.multiple_of(c * CHUNK, CHUNK)
            bgc = ubg[pl.ds(r0, CHUNK), :]
            gkc = ugk[pl.ds(r0, CHUNK), :]
            gvc = ugv[pl.ds(r0, CHUNK), :]

            def row_body(t, acc):
                bgt = ubg[pl.ds(r0 + t, 1), :]
                gqt = ugq[pl.ds(r0 + t, 1), :] * (gdk ** -0.5)
                valid = s_idx <= t
                e = jnp.exp(jnp.where(valid, bgt - bgc, 0.0))
                g = jnp.where(valid, gqt * gkc * e, 0.0).astype(BF16)
                a_cols = _dot(g, ind).astype(BF16)
                res = _dot_tn(a_cols, gvc)
                row = jnp.concatenate(
                    [res[hh:hh + 1, hh * gdv:(hh + 1) * gdv] for hh in range(GLA_HEADS)], axis=1)
                return jnp.where(t_idx == t, row, acc)

            gintra_s[pl.ds(r0, CHUNK), :] = lax.fori_loop(
                0, CHUNK, row_body, jnp.zeros((CHUNK, gwv), F32))
            return carry

        lax.fori_loop(0, nc, chunk_body, 0)

    def stage_b(x_ref, raw, exact_intra, a_pieces, normalise_next):
        a_pieces = list(a_pieces)
        for _ in range(len(a_pieces) // 2):
            a_pieces.pop(0)()
        uqk, v1_s, uo, gq_s, gk_s, gv_s, ugr, usm, bg_s = raw
        gate_s[:, 0:mw] = _sigmoid(uo[...])
        gr = ugr[...]
        gate_s[:, mw:mw + gwv] = gr * _sigmoid(gr)
        small = usm[...] + sbias_ref[...]

        b = pltpu.roll(_segment_scan(_log_sigmoid(small), CHUNK, jnp.add, 0.0),
                       LANES - MLSTM_HEADS, axis=1)
        a = small - b
        cm = _segment_scan(a, CHUNK, jnp.maximum, -jnp.inf)
        b_s[...] = b
        cm_s[...] = cm
        b_last = b_s[pl.ds(CHUNK - 1, nc, stride=CHUNK), :]
        cm_last = cm_s[pl.ds(CHUNK - 1, nc, stride=CHUNK), :]
        m = m_s[0:1, :]
        m_rows = []
        lw_rows = []
        for c in range(nc):
            bl = b_last[c:c + 1, :]
            gmax = bl + cm_last[c:c + 1, :]
            m_new = jnp.maximum(bl + m, gmax)
            dec_s[c:c + 1, :] = jnp.exp(bl + m - m_new)
            m_rows.append(jnp.broadcast_to(m, (CHUNK, LANES)))
            lw_rows.append(jnp.broadcast_to(gmax - m_new - cm_last[c:c + 1, :], (CHUNK, LANES)))
            m = m_new
        m_s[0:1, :] = m
        mc = jnp.concatenate(m_rows, axis=0)
        mx = jnp.maximum(cm, mc)
        mx_s[...] = mx
        emt_s[...] = jnp.exp(-(b + mx))
        at_s[...] = a.T
        wint = jnp.exp(mc - mx)
        wkey = jnp.exp(a + jnp.concatenate(lw_rows, axis=0))

        uqk[0:8, :] = carry_s[...]
        y = convb_ref[...] + uqk[pl.ds(8, ts), :] * convw_ref[3:4, :]
        y = y + uqk[pl.ds(7, ts), :] * convw_ref[2:3, :]
        y = y + uqk[pl.ds(6, ts), :] * convw_ref[1:2, :]
        y = y + uqk[pl.ds(5, ts), :] * convw_ref[0:1, :]
        carry_s[...] = uqk[pl.ds(ts, 8), :]
        y = y * _sigmoid(y)
        for hh in range(MLSTM_HEADS):
            cols = slice(hh * hd, (hh + 1) * hd)
            qf = y[:, hh * hd:(hh + 1) * hd] * (hd ** -0.5)
            kf = y[:, mw + hh * hd:mw + (hh + 1) * hd]
            q_s[:, cols] = qf.astype(BF16)
            qw_s[:, cols] = (qf * wint[:, hh:hh + 1]).astype(BF16)
            k_s[:, cols] = kf.astype(BF16)
            kw_s[:, cols] = (kf * wkey[:, hh:hh + 1]).astype(BF16)

        ti = lax.broadcasted_iota(jnp.int32, (CHUNK, CHUNK), 0)
        si = lax.broadcasted_iota(jnp.int32, (CHUNK, CHUNK), 1)
        causal = ti >= si
        lane = lax.broadcasted_iota(jnp.int32, (CHUNK, gwk), 1)

        def level1(c):
            rows = pl.ds(c * CHUNK, CHUNK)
            dec = dec_s[c:c + 1, :]
            qk = []
            rhs = []
            zk = jnp.zeros((CHUNK, hd), BF16)
            for h0 in range(0, MLSTM_HEADS, 2):
                h1 = h0 + 1
                k_blocks = jnp.concatenate(
                    [jnp.concatenate([k_s[rows, h0 * hd:h1 * hd], zk], axis=1),
                     jnp.concatenate([zk, k_s[rows, h1 * hd:(h1 + 1) * hd]], axis=1)], axis=0)
                qk_pair = _dot_nt(q_s[rows, h0 * hd:(h1 + 1) * hd], k_blocks)
                qk += [qk_pair[:, 0:CHUNK], qk_pair[:, CHUNK:2 * CHUNK]]
            for hh in range(MLSTM_HEADS):
                cols = slice(hh * hd, (hh + 1) * hd)
                v1h = v1_s[rows, 2 * hh * hd:(2 * hh + 2) * hd]
                c_prev = c_s[hh]
                rhs.append(jnp.concatenate([c_prev.astype(BF16), v1h], axis=0))
                c_s[hh] = dec[:, hh:hh + 1] * c_prev + _dot_tn(kw_s[rows, cols], v1h)
            bg = bg_s[rows, :]
            bg_last = bg[CHUNK - 1:CHUNK, :]
            gq = gq_s[rows, :]
            gk = gk_s[rows, :]
            gv = gv_s[rows, :]
            qe = gq * jnp.exp(bg) * (gdk ** -0.5)
            ke = (gk * jnp.exp(-bg)).astype(BF16)
            kw = gk * jnp.exp(bg_last - bg)
            st_prev = st_s[...]
            qe_heads = []
            kw_heads = []
            for hh in range(GLA_HEADS):
                hm = (lane >= hh * gdk) & (lane < (hh + 1) * gdk)
                qe_heads.append(jnp.where(hm, qe, 0.0).astype(BF16))
                kw_heads.append(jnp.where(hm, kw, 0.0).astype(BF16))
            qe_stack = jnp.concatenate(qe_heads, axis=0)
            kw_stack = jnp.concatenate(kw_heads, axis=0)
            v_stack = jnp.concatenate(
                [gv[:, hh * gdv:(hh + 1) * gdv] for hh in range(GLA_HEADS)], axis=0)
            a_all = _dot_nt(qe_stack, ke)
            inter_all = _dot_nt(qe_stack, st_prev.astype(BF16))
            st_s[...] = st_prev * jnp.exp(bg_last) + _dot_tn(v_stack, kw_stack)
            return qk, rhs, a_all, inter_all

        gains = jnp.concatenate([mng_ref[...], gng_ref[...]], axis=1)

        def finish_head(rows, cols, hv):
            normed = hv * lax.rsqrt(jnp.mean(hv * hv, axis=-1, keepdims=True) + EPS)
            mix_s[rows, cols] = (normed * gains[:, cols] * gate_s[rows, cols]).astype(BF16)

        def level2(c, qk, rhs, a_all, inter_all):
            rows = pl.ds(c * CHUNK, CHUNK)
            mxc = mx_s[rows, :]
            em = emt_s[rows, :]
            gv = gv_s[rows, :]
            for hh in range(MLSTM_HEADS):
                cols = slice(hh * hd, (hh + 1) * hd)
                col = slice(hh, hh + 1)
                a_row = at_s[hh:hh + 1, c * CHUNK:(c + 1) * CHUNK]
                pm = jnp.exp(jnp.where(causal, a_row - mxc[:, col], -jnp.inf))
                lhs = jnp.concatenate([qw_s[rows, cols], (qk[hh] * pm).astype(BF16)], axis=1)
                r = _dot(lhs, rhs[hh])
                finish_head(rows, cols,
                            r[:, 0:hd] / jnp.maximum(jnp.abs(r[:, hd:2 * hd]), em[:, col]))
            for hh in range(GLA_HEADS):
                hrows = slice(hh * CHUNK, (hh + 1) * CHUNK)
                gcols = slice(hh * gdv, (hh + 1) * gdv)
                a_h = jnp.where(causal, a_all[hrows, :], 0.0).astype(BF16)
                intra = jnp.where(exact_intra, gintra_s[rows, gcols], _dot(a_h, gv[:, gcols]))
                finish_head(rows, slice(mw + hh * gdv, mw + (hh + 1) * gdv),
                            intra + inter_all[hrows, :])

        def next_a_piece():
            if a_pieces:
                a_pieces.pop(0)()

        keep_for_tail = 2
        pending = level1(0)
        for c in range(nc):
            current = pending
            if c + 1 < nc:
                pending = level1(c + 1)
                for _ in range(-(-(len(a_pieces) - keep_for_tail) // (nc - 1 - c))):
                    next_a_piece()
            level2(c, *current)
            if c == nc // 4:
                normalise_next()

        while a_pieces:
            a_pieces.pop(0)()

        o_ref[0] = x_ref[0] + _dot(mix_s[...], wout_ref[...])

    j = pl.program_id(0)

    @pl.when(j == 0)
    def _():
        for raw in raw_sets:
            for hh in range(MLSTM_HEADS):
                raw[1][:, (2 * hh + 1) * hd:(2 * hh + 2) * hd] = jnp.ones((ts, hd), BF16)
        gintra_s[...] = jnp.zeros(gintra_s.shape, F32)
        flag_s[1] = jnp.int32(0)
        normalise(xb_ref, 0)
        for piece in stage_a_pieces(raw_sets[0], 0):
            piece()
        normalise(xa_ref, 1)

    @pl.when((j > 0) & ((j - 1) % nt == 0))
    def _():
        carry_s[...] = jnp.zeros(carry_s.shape, F32)
        c_s[...] = jnp.zeros(c_s.shape, F32)
        m_s[...] = jnp.zeros(m_s.shape, F32)
        st_s[...] = jnp.zeros(st_s.shape, F32)

    for parity in (0, 1):
        prev = 1 - parity

        @pl.when((j > 0) & (j % 2 == parity) & (flag_s[prev] != 0))
        def _():
            gla_intra_exact(raw_sets[prev])

        @pl.when((j > 0) & (j % 2 == parity))
        def _():
            stage_b(xb_ref, raw_sets[prev], flag_s[prev] != 0,
                    stage_a_pieces(raw_sets[parity], parity),
                    functools.partial(normalise, xa_ref, prev))


def _mixer(x, g, w_cat, sbias, conv_w, conv_b, wa2_pad, ba, mng, gng, w_out, *, ts, mw, gwk, gwv):
    B, S, D = x.shape
    ncol = w_cat.shape[1]
    nc = ts // CHUNK
    nt = S // ts
    n_tiles = B * nt
    hd = mw // MLSTM_HEADS
    gdv = gwv // GLA_HEADS
    assert gdv == hd and hd == LANES and gwk % LANES == 0 and S % ts == 0 and ts % (2 * CHUNK) == 0
    const = lambda j: (0, 0)

    def tile_a(j):
        t = jnp.minimum(j + 1, n_tiles - 1)
        return (t // nt, t % nt, 0)

    def tile_b(j):
        t = jnp.maximum(j - 1, 0)
        return (t // nt, t % nt, 0)

    single = pl.Buffered(1)
    col_f32 = pltpu.VMEM((ts, LANES), F32)
    raw_set = [
        pltpu.VMEM((ts + 8, 2 * mw), F32),
        pltpu.VMEM((ts, 2 * mw), BF16),
        pltpu.VMEM((ts, mw), F32),
        pltpu.VMEM((ts, gwk), F32),
        pltpu.VMEM((ts, gwk), F32),
        pltpu.VMEM((ts, gwv), BF16),
        pltpu.VMEM((ts, gwv), F32),
        col_f32,
        pltpu.VMEM((ts, gwk), F32),
    ]
    assert len(raw_set) == N_RAW
    kern = functools.partial(_mixer_kernel, ts=ts, nt=nt, mw=mw, gwk=gwk, gwv=gwv)
    return pl.pallas_call(
        kern,
        grid=(n_tiles + 1,),
        in_specs=[
            pl.BlockSpec((1, ts, D), tile_a),
            pl.BlockSpec((1, ts, D), tile_b),
            pl.BlockSpec((1, D), const),
            pl.BlockSpec((D, ncol), const, pipeline_mode=single),
            pl.BlockSpec((1, LANES), const),
            pl.BlockSpec((CONV_W, 2 * mw), const),
            pl.BlockSpec((1, 2 * mw), const),
            pl.BlockSpec((LANES, gwk), const),
            pl.BlockSpec((1, gwk), const),
            pl.BlockSpec((1, mw), const),
            pl.BlockSpec((1, gwv), const),
            pl.BlockSpec((mw + gwv, D), const, pipeline_mode=single),
        ],
        out_specs=pl.BlockSpec((1, ts, D), tile_b),
        out_shape=jax.ShapeDtypeStruct((B, S, D), F32),
        scratch_shapes=raw_set + raw_set + [
            pltpu.SMEM((2,), jnp.int32),
            pltpu.VMEM((ts, gwv), F32),
            pltpu.VMEM((2, ts, D), BF16),
            pltpu.VMEM((8, 2 * mw), F32),
            pltpu.VMEM((ts, mw), BF16),
            pltpu.VMEM((ts, mw), BF16),
            pltpu.VMEM((ts, mw), BF16),
            pltpu.VMEM((ts, mw), BF16),
            pltpu.VMEM((ts, mw + gwv), F32),
            pltpu.VMEM((ts, mw + gwv), BF16),
            col_f32,
            col_f32,
            pltpu.VMEM((LANES, ts), F32),
            col_f32,
            col_f32,
            pltpu.VMEM((max(nc, 8), LANES), F32),
            pltpu.VMEM((MLSTM_HEADS, hd, 2 * hd), F32),
            pltpu.VMEM((8, LANES), F32),
            pltpu.VMEM((gdv, gwk), F32),
        ],
        compiler_params=pltpu.CompilerParams(
            dimension_semantics=("arbitrary",), vmem_limit_bytes=VMEM_LIMIT_BYTES),
        name="mixer",
    )(x, x, g, w_cat, sbias, conv_w, conv_b, wa2_pad, ba, mng, gng, w_out)


def _attn_mlp_kernel(x_ref, k_ref, v_ref, xg_ref, wq_ref, wo_ref, mg_ref, w1_ref, w2_ref, fg_ref,
                     o_ref, *, ff_chunk):
    x = x_ref[...]
    D = x.shape[1]
    hd = D // XATTN_HEADS
    hq = _rmsnorm(x, xg_ref[...]).astype(BF16)
    q = (_dot(hq, wq_ref[...]) * (hd ** -0.5)).astype(BF16)
    atts = []
    for hh in range(XATTN_HEADS):
        cols = slice(hh * hd, (hh + 1) * hd)
        s = _dot_nt(q[:, cols], k_ref[0, :, cols])
        p = jnp.exp(s - jnp.max(s, axis=-1, keepdims=True))
        att = _dot(p.astype(BF16), v_ref[0, :, cols])
        atts.append((att / jnp.sum(p, axis=-1, keepdims=True)).astype(BF16))
    x = x + _dot(jnp.concatenate(atts, axis=1), wo_ref[...])

    hm = _rmsnorm(x, mg_ref[...]).astype(BF16)
    acc = x
    for c in range(w1_ref.shape[1] // ff_chunk):
        cols = slice(c * ff_chunk, (c + 1) * ff_chunk)
        t = jnp.maximum(_dot(hm, w1_ref[:, cols]), 0.0)
        acc = acc + _dot((t * t).astype(BF16), w2_ref[cols, :])
    o_ref[...] = _rmsnorm(acc, fg_ref[...])


def _attn_mlp(x2d, k, v, xg, wq, wo, mg, w1, w2, fg, *, tm, seq, ff_chunk):
    T, D = x2d.shape
    M = k.shape[1]
    F = w1.shape[1]
    assert T % tm == 0 and seq % tm == 0 and F % ff_chunk == 0
    per_b = seq // tm
    const = lambda i: (0, 0)
    single = pl.Buffered(1)
    kern = functools.partial(_attn_mlp_kernel, ff_chunk=ff_chunk)
    return pl.pallas_call(
        kern,
        grid=(T // tm,),
        in_specs=[
            pl.BlockSpec((tm, D), lambda i: (i, 0)),
            pl.BlockSpec((1, M, D), lambda i: (i // per_b, 0, 0)),
            pl.BlockSpec((1, M, D), lambda i: (i // per_b, 0, 0)),
            pl.BlockSpec((1, D), const),
            pl.BlockSpec((D, D), const, pipeline_mode=single),
            pl.BlockSpec((D, D), const, pipeline_mode=single),
            pl.BlockSpec((1, D), const),
            pl.BlockSpec((D, F), const, pipeline_mode=single),
            pl.BlockSpec((F, D), const, pipeline_mode=single),
            pl.BlockSpec((1, D), const),
        ],
        out_specs=pl.BlockSpec((tm, D), lambda i: (i, 0)),
        out_shape=jax.ShapeDtypeStruct((T, D), F32),
        compiler_params=pltpu.CompilerParams(
            dimension_semantics=("arbitrary",), vmem_limit_bytes=VMEM_LIMIT_BYTES),
        name="attn_mlp",
    )(x2d, k, v, xg, wq, wo, mg, w1, w2, fg)


MEM_KV_ROWS = 1024
MIXER_SEQ_TILE = 512
ATTN_MLP_TILE = 1024
FF_CHUNK = 1024


def kernel(x, mem, mix_norm_g, w_in, conv_w, conv_b, mlstm_i_b, mlstm_f_b, mlstm_norm_g, gla_wa2, gla_ba,
           gla_norm_g, w_out, xattn_norm_g, mem_norm_g, wq_x, wk_x, wv_x, wo_x, mlp_norm_g, w1, w2,
           final_norm_g):
    assert w_in.shape[0] == 1, "single-layer block"
    B, S, D = x.shape
    mw = mlstm_norm_g.shape[1]
    gwv = gla_norm_g.shape[1]
    rank, gwk = gla_wa2.shape[1:]
    nh = mlstm_i_b.shape[1]
    assert nh == MLSTM_HEADS and rank == GLA_RANK
    w_in = lax.optimization_barrier(w_in[0].astype(BF16))

    o_mi = 4 * mw
    o_gq = o_mi + 2 * nh
    o_ga = o_gq + 2 * gwk + 2 * gwv
    small_w = jnp.concatenate(
        [w_in[:, o_mi:o_gq], w_in[:, o_ga:o_ga + rank],
         jnp.zeros((D, LANES - 2 * nh - rank), w_in.dtype)], axis=1)
    w_cat = jnp.concatenate([w_in[:, 0:o_mi], w_in[:, o_gq:o_ga], small_w], axis=1)
    sbias = jnp.concatenate([mlstm_i_b[0], mlstm_f_b[0], jnp.zeros((LANES - 2 * nh,), F32)])[None, :]
    wa2_pad = jnp.zeros((LANES, gwk), F32).at[2 * nh:2 * nh + rank].set(gla_wa2[0]).astype(BF16)

    k_mem, v_mem = _mem_kv(mem, mem_norm_g, wk_x[0].astype(BF16), wv_x[0].astype(BF16),
                           rows=MEM_KV_ROWS)
    x = _mixer(x, mix_norm_g, w_cat, sbias, conv_w[0], conv_b, wa2_pad, gla_ba,
               mlstm_norm_g, gla_norm_g, w_out[0].astype(BF16),
               ts=MIXER_SEQ_TILE, mw=mw, gwk=gwk, gwv=gwv)
    out = _attn_mlp(x.reshape(B * S, D), k_mem, v_mem, xattn_norm_g, wq_x[0].astype(BF16),
                    wo_x[0].astype(BF16), mlp_norm_g, w1[0].astype(BF16), w2[0].astype(BF16),
                    final_norm_g[None, :], tm=ATTN_MLP_TILE, seq=S, ff_chunk=FF_CHUNK)
    return out.reshape(B, S, D)
```
